```python
import math
import jax
import jax.numpy as jnp
from jax import lax
import numpy as np

D_MODEL = 1024
BATCH = 2
SEQ = 16384
DEPTH = 2

H_A = 8
DK_A = 64
DV_A = 64
CONV_K = 4
CHUNK = 64
H_B = 8
H_B_KV = 2
DH_B = 64
WINDOW = 128
ATT_BLK = 128
NUM_BUCKETS = 32
MAX_DISTANCE = 128
POOL_WINDOWS = (2, 4, 8, 16)
N_POOL_GROUPS = 4
POOL_GROUP_DIM = D_MODEL // N_POOL_GROUPS
N_GROUPS = 4
EXPERTS_PER_GROUP = 8
N_EXPERTS = N_GROUPS * EXPERTS_PER_GROUP
TOP_K_INNER = 2
D_FF_EXPERT = D_MODEL // 2
MOE_BLK = 128
EPS = 1e-6

WA = H_A * DK_A
VA = H_A * DV_A
WB = H_B * DH_B
WKV_B = H_B_KV * DH_B
IN_SIZES = (WA, WA, VA, VA, H_A, H_A, WB, WKV_B, WKV_B)
P_IN = 2 * WA + 2 * VA + 2 * H_A + WB + 2 * WKV_B

kernel_name = 'hybrid_deltanet_swa_pool_hmoe'


def rms_norm(x, w):
    x32 = x.astype(jnp.float32)
    y = x32 * lax.rsqrt(jnp.mean(x32 * x32, axis=-1, keepdims=True) + EPS)
    return (y * w.astype(jnp.float32)).astype(x.dtype)


def l2_normalize(x):
    return x * lax.rsqrt(jnp.sum(x * x, axis=-1, keepdims=True) + EPS)


def causal_depthwise_conv(x, w):
    ch = x.shape[-1]
    return lax.conv_general_dilated(x, w[:, None, :].astype(x.dtype), window_strides=(1,), padding=((CONV_K - 1, 0),), dimension_numbers=('NWC', 'WIO', 'NWC'), feature_group_count=ch)


def t5_causal_bucket(dist):
    n = jnp.maximum(dist, 0)
    max_exact = NUM_BUCKETS // 2
    nf = jnp.maximum(n, 1).astype(jnp.float32)
    large = max_exact + (jnp.log(nf / max_exact) / math.log(MAX_DISTANCE / max_exact) * (NUM_BUCKETS - max_exact)).astype(jnp.int32)
    large = jnp.minimum(large, NUM_BUCKETS - 1)
    return jnp.where(n < max_exact, n, large)


def chunk_gated_delta_rule(q, k, v, g, beta):
    bsz, nh, seq, dk = q.shape
    dv = v.shape[-1]
    n = seq // CHUNK
    q = q.reshape(bsz, nh, n, CHUNK, dk)
    k = k.reshape(bsz, nh, n, CHUNK, dk)
    v = v.reshape(bsz, nh, n, CHUNK, dv)
    g = g.reshape(bsz, nh, n, CHUNK)
    beta = beta.reshape(bsz, nh, n, CHUNK)
    gc = jnp.cumsum(g, axis=-1)
    incl = jnp.tril(jnp.ones((CHUNK, CHUNK), dtype=bool))
    strict = jnp.tril(jnp.ones((CHUNK, CHUNK), dtype=bool), -1)
    decay = jnp.where(incl, jnp.exp(jnp.where(incl, gc[..., :, None] - gc[..., None, :], 0.0)), 0.0)
    kb = k * beta[..., None]
    a_mat = jnp.eye(CHUNK, dtype=jnp.float32) + jnp.where(strict, jnp.einsum('bhnid,bhnjd->bhnij', kb, k) * decay, 0.0)
    rhs = jnp.concatenate([v * beta[..., None], kb * jnp.exp(gc)[..., None]], axis=-1)
    sol = lax.linalg.triangular_solve(a_mat, rhs, left_side=True, lower=True, unit_diagonal=True)
    u, w = sol[..., :dv], sol[..., dv:]
    a_qk = jnp.einsum('bhnid,bhnjd->bhnij', q, k) * decay
    q_dec = q * jnp.exp(gc)[..., None]
    k_dec = k * jnp.exp(gc[..., -1:] - gc)[..., None]
    g_tot = jnp.exp(gc[..., -1])

    def step(state, inp):
        a_n, qd, kd, u_n, w_n, gt = inp
        v_new = u_n - jnp.einsum('bhck,bhkv->bhcv', w_n, state)
        o = jnp.einsum('bhck,bhkv->bhcv', qd, state) + jnp.einsum('bhij,bhjv->bhiv', a_n, v_new)
        state = state * gt[..., None, None] + jnp.einsum('bhck,bhcv->bhkv', kd, v_new)
        return state, o

    xs = (jnp.moveaxis(a_qk, 2, 0), jnp.moveaxis(q_dec, 2, 0), jnp.moveaxis(k_dec, 2, 0), jnp.moveaxis(u, 2, 0), jnp.moveaxis(w, 2, 0), jnp.moveaxis(g_tot, 2, 0))
    s0 = jnp.zeros((bsz, nh, dk, dv), jnp.float32)
    _, o = lax.scan(step, s0, xs)
    return jnp.moveaxis(o, 0, 2).reshape(bsz, nh, seq, dv)


def sliding_window_sink_attention(q, k, v, sinks, rel_bias):
    bsz, seq, _ = q.shape
    nb = seq // ATT_BLK
    grp = H_B // H_B_KV
    qb = q.reshape(bsz, nb, ATT_BLK, H_B_KV, grp, DH_B).astype(jnp.float32) * (DH_B ** -0.5)

    def band(t):
        t = t.reshape(bsz, seq, H_B_KV, DH_B)
        tp = jnp.concatenate([jnp.zeros_like(t[:, :ATT_BLK]), t], axis=1).reshape(bsz, nb + 1, ATT_BLK, H_B_KV, DH_B)
        return jnp.concatenate([tp[:, :-1], tp[:, 1:]], axis=2).astype(jnp.float32)

    kw = band(k)
    vw = band(v)
    s = jnp.einsum('bnqhgd,bnkhd->bhgnqk', qb, kw)
    qi = jnp.arange(ATT_BLK)[:, None]
    ki = jnp.arange(2 * ATT_BLK)[None, :]
    dist = qi + ATT_BLK - ki
    bias = rel_bias.astype(jnp.float32)[t5_causal_bucket(dist)]
    bias = bias.transpose(2, 0, 1).reshape(H_B_KV, grp, 1, ATT_BLK, 2 * ATT_BLK)
    key_pos = jnp.arange(nb)[:, None, None] * ATT_BLK + ki[None] - ATT_BLK
    valid = (dist >= 0) & (dist < WINDOW) & (key_pos >= 0)
    s = jnp.where(valid, s + bias, -1e30)
    sink = sinks.astype(jnp.float32).reshape(H_B_KV, grp, 1, 1, 1)
    m = jnp.maximum(jnp.max(s, axis=-1, keepdims=True), sink)
    p = jnp.exp(s - m)
    denom = jnp.sum(p, axis=-1, keepdims=True) + jnp.exp(sink - m)
    o = jnp.einsum('bhgnqk,bnkhd->bnqhgd', p / denom, vw)
    return o.reshape(bsz, seq, WB).astype(q.dtype)


def deltanet_swa_mixer(h, w_in, conv_w, a_log, dt_bias, onorm_w, sinks, rel_bias, w_out):
    bsz, seq, _ = h.shape
    f32 = jnp.float32
    proj = h @ w_in
    cuts = np.cumsum(IN_SIZES)[:-1].tolist()
    q_a, k_a, v_a, z_a, b_a, a_a, q_b, k_b, v_b = jnp.split(proj, cuts, axis=-1)
    qkv = jax.nn.silu(causal_depthwise_conv(jnp.concatenate([q_a, k_a, v_a], axis=-1), conv_w))
    qa, ka, va = jnp.split(qkv, [WA, 2 * WA], axis=-1)
    qa = l2_normalize(qa.reshape(bsz, seq, H_A, DK_A).transpose(0, 2, 1, 3).astype(f32)) * (DK_A ** -0.5)
    ka = l2_normalize(ka.reshape(bsz, seq, H_A, DK_A).transpose(0, 2, 1, 3).astype(f32))
    va = va.reshape(bsz, seq, H_A, DV_A).transpose(0, 2, 1, 3).astype(f32)
    beta = jax.nn.sigmoid(b_a.astype(f32)).transpose(0, 2, 1)
    g = (-jnp.exp(a_log.astype(f32)) * jax.nn.softplus(a_a.astype(f32) + dt_bias.astype(f32))).transpose(0, 2, 1)
    o_a = chunk_gated_delta_rule(qa, ka, va, g, beta).transpose(0, 2, 1, 3)
    o_a = rms_norm(o_a, onorm_w) * jax.nn.silu(z_a.reshape(bsz, seq, H_A, DV_A).astype(f32))
    out_a = o_a.reshape(bsz, seq, VA).astype(h.dtype)
    out_b = sliding_window_sink_attention(q_b, k_b, v_b, sinks, rel_bias)
    return jnp.concatenate([out_a, out_b], axis=-1) @ w_out


def multiscale_pool_mixer(h, pool_w, pool_scale):
    bsz, seq, d = h.shape
    f32 = jnp.float32
    hg = h.astype(f32).reshape(bsz, seq, N_POOL_GROUPS, POOL_GROUP_DIM)
    cs = jnp.cumsum(hg, axis=1)
    t = jnp.arange(seq)
    pooled = []
    for gi, win in enumerate(POOL_WINDOWS):
        hi = cs[:, :, gi]
        lo = jnp.concatenate([jnp.zeros((bsz, win, POOL_GROUP_DIM), f32), cs[:, :seq - win, gi]], axis=1)
        count = jnp.minimum(t + 1, win).astype(f32)[None, :, None]
        pooled.append((hi - lo) / count - hg[:, :, gi])
    pooled = jnp.stack(pooled, axis=2).astype(h.dtype)
    y = jnp.einsum('blgc,gcd->blgd', pooled, pool_w).reshape(bsz, seq, d)
    return y * pool_scale


def hierarchical_moe(h, r1_w, r1_b, r2_w, r2_b, w_gate, w_up, w_down):
    bsz, seq, d = h.shape
    n_tok = bsz * seq
    xf = h.reshape(n_tok, d)
    p_grp = jax.nn.softmax((xf @ r1_w).astype(jnp.float32) + r1_b.astype(jnp.float32), axis=-1)
    p_top, grp = lax.top_k(p_grp, 1)
    p_top = p_top[:, 0]
    grp = grp[:, 0]
    lg2 = ((xf @ r2_w).astype(jnp.float32) + r2_b.astype(jnp.float32)).reshape(n_tok, N_GROUPS, EXPERTS_PER_GROUP)
    lg2 = lg2[jnp.arange(n_tok), grp]
    top_l, top_e = lax.top_k(lg2, TOP_K_INNER)
    gate = p_top[:, None] * jax.nn.softmax(top_l, axis=-1)
    expert = grp[:, None] * EXPERTS_PER_GROUP + top_e
    n_asg = n_tok * TOP_K_INNER
    e_flat = expert.reshape(n_asg)
    g_flat = gate.reshape(n_asg)
    tok_flat = jnp.arange(n_asg, dtype=jnp.int32) // TOP_K_INNER
    order = jnp.argsort(e_flat)
    e_s = e_flat[order]
    g_s = g_flat[order]
    tok_s = tok_flat[order]
    counts = jnp.zeros((N_EXPERTS,), jnp.int32).at[e_flat].add(1)
    starts = jnp.cumsum(counts) - counts
    padded = (counts + MOE_BLK - 1) // MOE_BLK * MOE_BLK
    pad_end = jnp.cumsum(padded)
    pad_start = pad_end - padded
    dest = pad_start[e_s] + (jnp.arange(n_asg, dtype=jnp.int32) - starts[e_s])
    n_rows = (n_asg + N_EXPERTS * (MOE_BLK - 1) + MOE_BLK - 1) // MOE_BLK * MOE_BLK
    n_blk = n_rows // MOE_BLK
    row_tok = jnp.full((n_rows,), n_tok, jnp.int32).at[dest].set(tok_s)
    row_gate = jnp.zeros((n_rows,), jnp.float32).at[dest].set(g_s)
    blk_expert = jnp.minimum(jnp.searchsorted(pad_end, jnp.arange(n_blk, dtype=jnp.int32) * MOE_BLK, side='right'), N_EXPERTS - 1)
    x_pad = jnp.concatenate([xf, jnp.zeros((1, d), xf.dtype)], axis=0)
    x_rows = x_pad[row_tok].reshape(n_blk, MOE_BLK, d)

    def expert_block(args):
        xb, e = args
        return (jax.nn.silu(xb @ w_gate[e]) * (xb @ w_up[e])) @ w_down[e]

    y_rows = lax.map(expert_block, (x_rows, blk_expert)).reshape(n_rows, d)
    y = jax.ops.segment_sum(y_rows * row_gate[:, None].astype(y_rows.dtype), row_tok, num_segments=n_tok + 1)[:n_tok]
    return y.reshape(bsz, seq, d)


def setup_inputs(seed: int = 0) -> dict:
    key = jax.random.key(seed)
    ks = jax.random.split(key, 24)
    f32 = jnp.float32
    d = D_MODEL
    n_even = (DEPTH + 1) // 2
    n_odd = DEPTH // 2

    def nrm(k, shape, scale):
        return jax.random.normal(k, shape, f32) * scale

    x = nrm(ks[0], (BATCH, SEQ, d), 1.0)
    c = nrm(ks[1], (BATCH, d), 1.0)
    ada_w = nrm(ks[2], (DEPTH, d, 6 * d), 0.5 * d ** -0.5)
    ada_b = nrm(ks[3], (DEPTH, 6 * d), 0.02)
    norm_mix_w = 1.0 + nrm(ks[4], (DEPTH, d), 0.02)
    norm_ffn_w = 1.0 + nrm(ks[5], (DEPTH, d), 0.02)
    ab_w_in = nrm(ks[6], (n_even, d, P_IN), d ** -0.5)
    ab_conv_w = nrm(ks[7], (n_even, CONV_K, 3 * WA), CONV_K ** -0.5)
    ab_a_log = jnp.log(jax.random.uniform(ks[8], (n_even, H_A), f32, 1.0, 16.0))
    dt0 = jnp.exp(jax.random.uniform(ks[9], (n_even, H_A), f32, math.log(1e-3), math.log(1e-1)))
    ab_dt_bias = dt0 + jnp.log(-jnp.expm1(-dt0))
    ab_onorm_w = 1.0 + nrm(ks[10], (n_even, DV_A), 0.02)
    ab_sinks = nrm(ks[11], (n_even, H_B), 0.5)
    rel_bias = nrm(ks[12], (NUM_BUCKETS, H_B), 0.5)
    ab_w_out = nrm(ks[13], (n_even, VA + WB, d), (VA + WB) ** -0.5)
    pool_w = nrm(ks[14], (n_odd, N_POOL_GROUPS, POOL_GROUP_DIM, POOL_GROUP_DIM), POOL_GROUP_DIM ** -0.5)
    pool_scale = 1.0 + nrm(ks[15], (n_odd, d), 0.1)
    r1_w = nrm(ks[16], (DEPTH, d, N_GROUPS), d ** -0.5)
    r1_b = nrm(ks[17], (DEPTH, N_GROUPS), 0.01)
    r2_w = nrm(ks[18], (DEPTH, d, N_EXPERTS), d ** -0.5)
    r2_b = nrm(ks[19], (DEPTH, N_EXPERTS), 0.01)
    moe_w_gate = nrm(ks[20], (DEPTH, N_EXPERTS, d, D_FF_EXPERT), d ** -0.5)
    moe_w_up = nrm(ks[21], (DEPTH, N_EXPERTS, d, D_FF_EXPERT), d ** -0.5)
    moe_w_down = nrm(ks[22], (DEPTH, N_EXPERTS, D_FF_EXPERT, d), D_FF_EXPERT ** -0.5)
    final_norm_w = 1.0 + nrm(ks[23], (d,), 0.02)
    return {'x': x, 'c': c, 'ada_w': ada_w, 'ada_b': ada_b, 'norm_mix_w': norm_mix_w, 'norm_ffn_w': norm_ffn_w, 'ab_w_in': ab_w_in, 'ab_conv_w': ab_conv_w, 'ab_a_log': ab_a_log, 'ab_dt_bias': ab_dt_bias, 'ab_onorm_w': ab_onorm_w, 'ab_sinks': ab_sinks, 'rel_bias': rel_bias, 'ab_w_out': ab_w_out, 'pool_w': pool_w, 'pool_scale': pool_scale, 'r1_w': r1_w, 'r1_b': r1_b, 'r2_w': r2_w, 'r2_b': r2_b, 'moe_w_gate': moe_w_gate, 'moe_w_up': moe_w_up, 'moe_w_down': moe_w_down, 'final_norm_w': final_norm_w}


def reference(x, c, ada_w, ada_b, norm_mix_w, norm_ffn_w, ab_w_in, ab_conv_w, ab_a_log, ab_dt_bias, ab_onorm_w, ab_sinks, rel_bias, ab_w_out, pool_w, pool_scale, r1_w, r1_b, r2_w, r2_b, moe_w_gate, moe_w_up, moe_w_down, final_norm_w):
    cond = jax.nn.silu(c)
    for layer in range(DEPTH):
        mod = cond @ ada_w[layer] + ada_b[layer]
        sh1, sc1, g1, sh2, sc2, g2 = [m[:, None, :] for m in jnp.split(mod, 6, axis=-1)]
        h = rms_norm(x, norm_mix_w[layer]) * (1 + sc1) + sh1
        i = layer // 2
        if layer % 2 == 0:
            y = deltanet_swa_mixer(h, ab_w_in[i], ab_conv_w[i], ab_a_log[i], ab_dt_bias[i], ab_onorm_w[i], ab_sinks[i], rel_bias, ab_w_out[i])
        else:
            y = multiscale_pool_mixer(h, pool_w[i], pool_scale[i])
        x = x + g1 * y
        h = rms_norm(x, norm_ffn_w[layer]) * (1 + sc2) + sh2
        x = x + g2 * hierarchical_moe(h, r1_w[layer], r1_b[layer], r2_w[layer], r2_b[layer], moe_w_gate[layer], moe_w_up[layer], moe_w_down[layer])
    return rms_norm(x, final_norm_w)
```

```python
import functools
import math

import jax
import jax.numpy as jnp
from jax import lax
from jax.experimental import pallas as pl
from jax.experimental.pallas import tpu as pltpu

F32 = jnp.float32
BF16 = jnp.bfloat16
I32 = jnp.int32

D_MODEL = 1024
DEPTH = 2
H_A = 8
DK_A = 64
CONV_K = 4
CHUNK = 64
H_B = 8
H_B_KV = 2
DH_B = 64
WINDOW = 128
ATT_BLK = 128
NUM_BUCKETS = 32
MAX_DISTANCE = 128
POOL_WINDOWS = (2, 4, 8, 16)
POOL_GROUP_DIM = D_MODEL // 4
POOL_HIST = 32
N_GROUPS = 4
EXPERTS_PER_GROUP = 8
N_EXPERTS = N_GROUPS * EXPERTS_PER_GROUP
D_FF = D_MODEL // 2
EPS = 1e-6

WA = H_A * DK_A
WB = H_B * DH_B
WKV = H_B_KV * DH_B
N_PAIR = H_A // 2
LANES = 128
MOE_ROWS = 256
VMEM_LIMIT = 56 * 1024 * 1024

M_E1, M_E2, M_R1, M_R2, M_G1, M_G2 = 0, 1, 2, 3, 4, 5
R_GRP0 = N_EXPERTS


def _cparams(sem):
    return pltpu.CompilerParams(dimension_semantics=sem, vmem_limit_bytes=VMEM_LIMIT)


def _mm(a, b):
    return jnp.dot(a.astype(BF16), b.astype(BF16), preferred_element_type=F32)


def _mm_nt(a, b):
    return lax.dot_general(a.astype(BF16), b.astype(BF16), (((1,), (1,)), ((), ())),
                           preferred_element_type=F32)


def _mm_x3(a, b):
    ah = a.astype(BF16)
    al = (a - ah.astype(F32)).astype(BF16)
    bh = b.astype(BF16)
    bl = (b - bh.astype(F32)).astype(BF16)
    return jnp.dot(jnp.concatenate([ah, al, ah], axis=1), jnp.concatenate([bh, bh, bl], axis=0),
                   preferred_element_type=F32)


def _split3(x):
    hi = x.astype(BF16)
    r = x - hi.astype(F32)
    mid = r.astype(BF16)
    lo = (r - mid.astype(F32)).astype(BF16)
    return hi, mid, lo


def _norm_mod(x, nw, sc, sh):
    ms = jnp.mean(x * x, axis=-1, keepdims=True)
    return (x * lax.rsqrt(ms + EPS) * nw) * (1.0 + sc) + sh


def _silu(x):
    return x * jax.nn.sigmoid(x)


def _ada_kernel(ct_ref, w_ref, b_ref, o_ref):
    nb = ct_ref.shape[1]
    d = ct_ref.shape[0]
    rows = []
    for b in range(nb):
        acc = None
        for k0 in range(0, d, LANES):
            ct = ct_ref[k0:k0 + LANES, b:b + 1]
            part = jnp.sum(_silu(ct) * w_ref[0, k0:k0 + LANES, :], axis=0, keepdims=True)
            acc = part if acc is None else acc + part
        rows.append(acc)
    o_ref[0] = jnp.concatenate(rows, axis=0) + b_ref[0]


def _ada(c, ada_w, ada_b):
    nb, d = c.shape
    depth, _, n6 = ada_w.shape
    cols = 1536
    return pl.pallas_call(
        _ada_kernel,
        grid=(depth, n6 // cols),
        in_specs=[
            pl.BlockSpec((d, nb), lambda l, j: (0, 0)),
            pl.BlockSpec((1, d, cols), lambda l, j: (l, 0, j)),
            pl.BlockSpec((1, 1, cols), lambda l, j: (l, 0, j)),
        ],
        out_specs=pl.BlockSpec((1, nb, cols), lambda l, j: (l, 0, j)),
        out_shape=jax.ShapeDtypeStruct((depth, nb, n6), F32),
        compiler_params=_cparams(("arbitrary", "arbitrary")),
        name="ada_mod",
    )(c.T, ada_w, ada_b.reshape(depth, 1, n6))


def _inproj_kernel(x_ref, mod_ref, nw_ref, w_ref, a_ref, b_ref, g_ref):
    h = _norm_mod(x_ref[...], nw_ref[...], mod_ref[0, 1:2, :], mod_ref[0, 0:1, :])
    hb = h.astype(BF16)
    na = a_ref.shape[1]
    nbb = b_ref.shape[1]
    a_ref[...] = jnp.dot(hb, w_ref[:, 0:na], preferred_element_type=F32)
    b_ref[...] = jnp.dot(hb, w_ref[:, na:na + nbb], preferred_element_type=F32).astype(BF16)
    g_ref[...] = jnp.dot(hb, w_ref[:, na + nbb:], preferred_element_type=F32)


def _inproj(x, mod, nw, w_all, seq, tm):
    t, d = x.shape
    tps = seq // tm
    na, nbb, ng = 4 * WA, WB + 2 * WKV, LANES
    return pl.pallas_call(
        _inproj_kernel,
        grid=(t // tm,),
        in_specs=[
            pl.BlockSpec((tm, d), lambda i: (i, 0)),
            pl.BlockSpec((1, 6, d), lambda i: (i // tps, 0, 0)),
            pl.BlockSpec((1, d), lambda i: (0, 0)),
            pl.BlockSpec((d, na + nbb + ng), lambda i: (0, 0)),
        ],
        out_specs=[
            pl.BlockSpec((tm, na), lambda i: (i, 0)),
            pl.BlockSpec((tm, nbb), lambda i: (i, 0)),
            pl.BlockSpec((tm, ng), lambda i: (i, 0)),
        ],
        out_shape=[
            jax.ShapeDtypeStruct((t, na), F32),
            jax.ShapeDtypeStruct((t, nbb), BF16),
            jax.ShapeDtypeStruct((t, ng), F32),
        ],
        compiler_params=_cparams(("arbitrary",)),
        name="inproj",
    )(x, mod, nw, w_all)


def _lane_half(shape):
    return lax.broadcasted_iota(I32, shape, 1) < (LANES // 2)


def _half_sums(x2, lo):
    se = jnp.sum(jnp.where(lo, x2, 0.0), axis=-1, keepdims=True)
    so = jnp.sum(jnp.where(lo, 0.0, x2), axis=-1, keepdims=True)
    return jnp.where(lo, se, so)


def _deltanet_kernel(a_ref, gp_ref, cw_ref, alog_ref, dt_ref, onw_ref, o_ref,
                     ext_ref, s_ref, q_ref, k_ref, v_ref, gf_ref, bf_ref, *, blocks_per_seq):
    lb = a_ref.shape[0]
    i = pl.program_id(0)

    @pl.when(i % blocks_per_seq == 0)
    def _():
        ext_ref[0:8, :] = jnp.zeros((8, ext_ref.shape[1]), F32)
        s_ref[...] = jnp.zeros(s_ref.shape, F32)

    lo_b = _lane_half((lb, LANES))
    for part, dst in ((0, q_ref), (1, k_ref), (2, v_ref)):
        for p in range(N_PAIR):
            c0 = part * WA + p * LANES
            xin = a_ref[:, c0:c0 + LANES]
            ext_ref[8:8 + lb, c0:c0 + LANES] = xin
            acc = cw_ref[CONV_K - 1:CONV_K, c0:c0 + LANES] * xin
            for j in range(CONV_K - 1):
                acc = acc + cw_ref[j:j + 1, c0:c0 + LANES] * ext_ref[pl.ds(8 - (CONV_K - 1) + j, lb), c0:c0 + LANES]
            ext_ref[0:8, c0:c0 + LANES] = xin[lb - 8:lb, :]
            y = _silu(acc)
            if part < 2:
                y = y * lax.rsqrt(_half_sums(y * y, lo_b) + EPS)
                if part == 0:
                    y = y * (DK_A ** -0.5)
            dst[:, p * LANES:(p + 1) * LANES] = y

    gp = gp_ref[...]
    bf_ref[...] = jax.nn.sigmoid(gp)
    xg = gp + dt_ref[...]
    softplus = jnp.maximum(xg, 0.0) + jnp.log(1.0 + jnp.exp(-jnp.abs(xg)))
    gf_ref[...] = -jnp.exp(alog_ref[...]) * softplus

    c = CHUNK
    c2 = 2 * c
    row = lax.broadcasted_iota(I32, (c2, c2), 0)
    col = lax.broadcasted_iota(I32, (c2, c2), 1)
    same_blk = (row < c) == (col < c)
    incl = same_blk & (row >= col)
    strict = same_blk & (row > col)
    eye = row == col
    tril_bd = jnp.where(incl, 1.0, 0.0).astype(BF16)
    eye_f = jnp.where(eye, 1.0, 0.0)
    lo = _lane_half((c, LANES))

    def stack(x):
        return jnp.where(same_blk, jnp.concatenate([x, x], axis=0), 0.0)

    def unstack(x):
        return jnp.where(lo, x[0:c], x[c:c2])

    def chunk_body(ci, carry):
        r0 = pl.multiple_of(ci * c, c)
        gfc = gf_ref[pl.ds(r0, c), :]
        bfc = bf_ref[pl.ds(r0, c), :]
        for p in range(N_PAIR):
            e, o = 2 * p, 2 * p + 1
            q = q_ref[pl.ds(r0, c), p * LANES:(p + 1) * LANES]
            k = k_ref[pl.ds(r0, c), p * LANES:(p + 1) * LANES]
            v = v_ref[pl.ds(r0, c), p * LANES:(p + 1) * LANES]
            beta = jnp.where(lo, jnp.broadcast_to(bfc[:, e:e + 1], (c, LANES)),
                             jnp.broadcast_to(bfc[:, o:o + 1], (c, LANES)))
            ge = jnp.broadcast_to(gfc[:, H_A + e:H_A + e + 1], (c, LANES))
            go = jnp.broadcast_to(gfc[:, H_A + o:H_A + o + 1], (c, LANES))
            g_hi, g_mid, g_lo = _split3(jnp.concatenate([ge, go], axis=0))
            gcol = (jnp.dot(tril_bd, g_hi, preferred_element_type=F32)
                    + jnp.dot(tril_bd, g_mid, preferred_element_type=F32)
                    + jnp.dot(tril_bd, g_lo, preferred_element_type=F32))
            grow = jnp.sum(jnp.where(eye, gcol, 0.0), axis=0, keepdims=True)
            decay = jnp.where(incl, jnp.exp(jnp.where(incl, gcol - grow, 0.0)), 0.0)
            gc = unstack(gcol)
            egc = jnp.exp(gc)
            kb = k * beta
            k_st = stack(k)
            n_mat = jnp.where(strict, _mm_nt(stack(kb), k_st) * decay, 0.0)
            a_qk = _mm_nt(stack(q), k_st) * decay
            x_inv = eye_f - n_mat
            y_pow = _mm_x3(n_mat, n_mat)
            for it in range(5):
                x_inv = x_inv + _mm_x3(x_inv, y_pow)
                if it < 4:
                    y_pow = _mm_x3(y_pow, y_pow)
            vb = v * beta
            kbg = kb * egc
            rhs = jnp.concatenate([jnp.concatenate([vb, vb], axis=0),
                                   jnp.concatenate([kbg, kbg], axis=0)], axis=1)
            sol = _mm(x_inv, rhs)
            u = unstack(sol[:, 0:LANES])
            w = unstack(sol[:, LANES:2 * LANES])
            state = s_ref[p]
            q_dec = q * egc
            wq = _mm(jnp.concatenate([w, q_dec], axis=0), state)
            v_new = u - wq[0:c]
            av = _mm(a_qk, jnp.concatenate([v_new, v_new], axis=0))
            out = wq[c:c2] + unstack(av)
            k_dec = k * jnp.exp(gc[c - 1:c, :] - gc)
            g_tot = jnp.where(row < c, jnp.exp(gcol[c - 1:c, :]), jnp.exp(gcol[c2 - 1:c2, :]))
            s_ref[p] = state * g_tot + jnp.where(same_blk, _mm(k_dec.T, v_new), 0.0)
            ms = _half_sums(out * out, lo) * (1.0 / DK_A)
            z = a_ref[pl.ds(r0, c), 3 * WA + p * LANES:3 * WA + (p + 1) * LANES]
            res = out * lax.rsqrt(ms + EPS) * onw_ref[...] * _silu(z)
            o_ref[pl.ds(r0, c), p * LANES:(p + 1) * LANES] = res.astype(BF16)
        return carry

    lax.fori_loop(0, lb // c, chunk_body, 0)


def _deltanet(a1, g1, conv_w, alog_row, dt_row, onw_row, seq, lb):
    t = a1.shape[0]
    kern = functools.partial(_deltanet_kernel, blocks_per_seq=seq // lb)
    return pl.pallas_call(
        kern,
        grid=(t // lb,),
        in_specs=[
            pl.BlockSpec((lb, 4 * WA), lambda i: (i, 0)),
            pl.BlockSpec((lb, LANES), lambda i: (i, 0)),
            pl.BlockSpec((CONV_K, 3 * WA), lambda i: (0, 0)),
            pl.BlockSpec((1, LANES), lambda i: (0, 0)),
            pl.BlockSpec((1, LANES), lambda i: (0, 0)),
            pl.BlockSpec((1, LANES), lambda i: (0, 0)),
        ],
        out_specs=pl.BlockSpec((lb, WA), lambda i: (i, 0)),
        out_shape=jax.ShapeDtypeStruct((t, WA), BF16),
        scratch_shapes=[
            pltpu.VMEM((lb + 8, 3 * WA), F32),
            pltpu.VMEM((N_PAIR, LANES, LANES), F32),
            pltpu.VMEM((lb, WA), F32),
            pltpu.VMEM((lb, WA), F32),
            pltpu.VMEM((lb, WA), F32),
            pltpu.VMEM((lb, LANES), F32),
            pltpu.VMEM((lb, LANES), F32),
        ],
        compiler_params=_cparams(("arbitrary",)),
        name="deltanet",
    )(a1, g1, conv_w, alog_row, dt_row, onw_row)


def _swa_kernel(sink_ref, cur_ref, prev_ref, bias_ref, o_ref, *, blocks_per_seq):
    blk = ATT_BLK
    i = pl.program_id(0)
    first = (i % blocks_per_seq) == 0
    row = lax.broadcasted_iota(I32, (2 * blk, 2 * blk), 0)
    col = lax.broadcasted_iota(I32, (2 * blk, 2 * blk), 1)
    qi = jnp.where(row < blk, row, row - blk)
    dist = qi + blk - col
    first_key = jnp.where(first, blk, 0)
    valid = (dist >= 0) & (dist < WINDOW) & (col >= first_key)
    lo_kv = _lane_half((2 * blk, LANES))
    lo_q = _lane_half((blk, LANES))
    rowq = lax.broadcasted_iota(I32, (2 * blk, LANES), 0)
    q_keep = (rowq < blk) == lo_kv
    top_rows = lax.broadcasted_iota(I32, (2 * blk, 1), 0) < blk

    k0 = WB
    v0 = WB + WKV
    kk = jnp.concatenate([prev_ref[:, k0:k0 + WKV], cur_ref[:, k0:k0 + WKV]], axis=0).astype(F32)
    vv = jnp.concatenate([prev_ref[:, v0:v0 + WKV], cur_ref[:, v0:v0 + WKV]], axis=0).astype(F32)
    kk_sw = pltpu.roll(kk, LANES // 2, axis=1)
    vv_sw = pltpu.roll(vv, LANES // 2, axis=1)
    k_dup = (jnp.where(lo_kv, kk, kk_sw), jnp.where(lo_kv, kk_sw, kk))
    v_dup = (jnp.where(lo_kv, vv, vv_sw), jnp.where(lo_kv, vv_sw, vv))

    for p in range(H_B // 2):
        j = (2 * p) // (H_B // H_B_KV)
        qp = cur_ref[:, p * LANES:(p + 1) * LANES].astype(F32) * (DH_B ** -0.5)
        q_st = jnp.where(q_keep, jnp.concatenate([qp, qp], axis=0), 0.0)
        s = _mm_nt(q_st, k_dup[j])
        s = jnp.where(valid, s + bias_ref[p], -1e30)
        sink = jnp.where(top_rows, sink_ref[2 * p], sink_ref[2 * p + 1])
        m = jnp.maximum(jnp.max(s, axis=-1, keepdims=True), sink)
        pe = jnp.exp(s - m)
        den = jnp.sum(pe, axis=-1, keepdims=True) + jnp.exp(sink - m)
        o2 = _mm(pe, v_dup[j]) / den
        o_ref[:, p * LANES:(p + 1) * LANES] = jnp.where(lo_q, o2[0:blk], o2[blk:2 * blk]).astype(BF16)


def _swa(b1, sinks, bias_tab, seq):
    t, nc = b1.shape
    nbs = seq // ATT_BLK
    kern = functools.partial(_swa_kernel, blocks_per_seq=nbs)
    return pl.pallas_call(
        kern,
        grid=(t // ATT_BLK,),
        in_specs=[
            pl.BlockSpec(memory_space=pltpu.SMEM),
            pl.BlockSpec((ATT_BLK, nc), lambda i: (i, 0)),
            pl.BlockSpec((ATT_BLK, nc), lambda i: (jnp.maximum(i - 1, 0), 0)),
            pl.BlockSpec((H_B // 2, 2 * ATT_BLK, 2 * ATT_BLK), lambda i: (0, 0, 0)),
        ],
        out_specs=pl.BlockSpec((ATT_BLK, WB), lambda i: (i, 0)),
        out_shape=jax.ShapeDtypeStruct((t, WB), BF16),
        compiler_params=_cparams(("arbitrary",)),
        name="swa",
    )(sinks, b1, b1, bias_tab)


def _t5_bias_table(rel_bias):
    qi = jnp.arange(ATT_BLK)[:, None]
    ki = jnp.arange(2 * ATT_BLK)[None, :]
    n = jnp.maximum(qi + ATT_BLK - ki, 0)
    max_exact = NUM_BUCKETS // 2
    nf = jnp.maximum(n, 1).astype(F32)
    large = max_exact + (jnp.log(nf / max_exact) / math.log(MAX_DISTANCE / max_exact)
                         * (NUM_BUCKETS - max_exact)).astype(I32)
    bucket = jnp.where(n < max_exact, n, jnp.minimum(large, NUM_BUCKETS - 1))
    bias = rel_bias.astype(F32)[bucket]
    return bias.transpose(2, 0, 1).reshape(H_B // 2, 2 * ATT_BLK, 2 * ATT_BLK)


def _route_tail(x1, mod_ref, nw2_ref, rwhl_ref, rwhi_ref, rb_ref, tril_ref, carry_ref,
                h2_ref, meta_ref, cnt_ref):
    tm = x1.shape[0]
    h2 = _norm_mod(x1, nw2_ref[...], mod_ref[0, 4:5, :], mod_ref[0, 3:4, :])
    h2_ref[...] = h2
    h_hi = h2.astype(BF16)
    h_lo = (h2 - h_hi.astype(F32)).astype(BF16)
    r = jnp.dot(h_hi, rwhl_ref[...], preferred_element_type=F32)
    logits = (r[:, 0:LANES] + r[:, LANES:2 * LANES]
              + jnp.dot(h_lo, rwhi_ref[...], preferred_element_type=F32) + rb_ref[...])
    lane = lax.broadcasted_iota(I32, (tm, LANES), 1)
    lane_f = lane.astype(F32)
    big = 1e9
    neg = -jnp.inf
    is_grp = (lane >= R_GRP0) & (lane < R_GRP0 + N_GROUPS)
    lg1 = jnp.where(is_grp, logits, neg)
    m1 = jnp.max(lg1, axis=-1, keepdims=True)
    grp = jnp.min(jnp.where(lg1 == m1, lane_f, big), axis=-1, keepdims=True) - R_GRP0
    p_top = 1.0 / jnp.sum(jnp.where(is_grp, jnp.exp(lg1 - m1), 0.0), axis=-1, keepdims=True)
    lane_grp = jnp.right_shift(lane, EXPERTS_PER_GROUP.bit_length() - 1).astype(F32)
    in_grp = (lane < N_EXPERTS) & (lane_grp == grp)
    l2 = jnp.where(in_grp, logits, neg)
    t1 = jnp.max(l2, axis=-1, keepdims=True)
    e1 = jnp.min(jnp.where(l2 == t1, lane_f, big), axis=-1, keepdims=True)
    l2b = jnp.where(lane_f == e1, neg, l2)
    t2 = jnp.max(l2b, axis=-1, keepdims=True)
    e2 = jnp.min(jnp.where(l2b == t2, lane_f, big), axis=-1, keepdims=True)
    ex = jnp.exp(t2 - t1)
    gate1 = p_top / (1.0 + ex)
    gate2 = p_top * ex / (1.0 + ex)
    hit1 = lane_f == e1
    hit2 = lane_f == e2
    onehot = jnp.where(hit1 | hit2, 1.0, 0.0)
    incl = jnp.dot(tril_ref[...], onehot.astype(BF16), preferred_element_type=F32)
    excl = incl - onehot + carry_ref[...]
    rank1 = jnp.sum(jnp.where(hit1, excl, 0.0), axis=-1, keepdims=True)
    rank2 = jnp.sum(jnp.where(hit2, excl, 0.0), axis=-1, keepdims=True)
    carry_ref[...] = carry_ref[...] + incl[tm - 1:tm, :]
    cnt_ref[...] = carry_ref[...]
    meta = jnp.zeros((tm, LANES), F32)
    for idx, val in ((M_E1, e1), (M_E2, e2), (M_R1, rank1), (M_R2, rank2), (M_G1, gate1), (M_G2, gate2)):
        meta = jnp.where(lane == idx, val, meta)
    meta_ref[...] = meta


def _tail_in_specs(d, tm, tps):
    return [
        pl.BlockSpec((1, 6, d), lambda i: (i // tps, 0, 0)),
        pl.BlockSpec((1, d), lambda i: (0, 0)),
        pl.BlockSpec((d, 2 * LANES), lambda i: (0, 0)),
        pl.BlockSpec((d, LANES), lambda i: (0, 0)),
        pl.BlockSpec((1, LANES), lambda i: (0, 0)),
        pl.BlockSpec((tm, tm), lambda i: (0, 0)),
    ]


def _tail_out(t, d, tm):
    specs = [
        pl.BlockSpec((tm, d), lambda i: (i, 0)),
        pl.BlockSpec((tm, d), lambda i: (i, 0)),
        pl.BlockSpec((tm, LANES), lambda i: (i, 0)),
        pl.BlockSpec((1, LANES), lambda i: (0, 0)),
    ]
    shapes = [
        jax.ShapeDtypeStruct((t, d), F32),
        jax.ShapeDtypeStruct((t, d), F32),
        jax.ShapeDtypeStruct((t, LANES), F32),
        jax.ShapeDtypeStruct((1, LANES), F32),
    ]
    return specs, shapes


def _outproj_kernel(oa_ref, ob_ref, x_ref, wo_ref, mod_ref, nw2_ref, rwhl_ref, rwhi_ref, rb_ref,
                    tril_ref, x1_ref, h2_ref, meta_ref, cnt_ref, carry_ref):
    @pl.when(pl.program_id(0) == 0)
    def _():
        carry_ref[...] = jnp.zeros(carry_ref.shape, F32)

    y = (jnp.dot(oa_ref[...], wo_ref[0:WA, :], preferred_element_type=F32)
         + jnp.dot(ob_ref[...], wo_ref[WA:WA + WB, :], preferred_element_type=F32))
    x1 = x_ref[...] + mod_ref[0, 2:3, :] * y
    x1_ref[...] = x1
    _route_tail(x1, mod_ref, nw2_ref, rwhl_ref, rwhi_ref, rb_ref, tril_ref, carry_ref,
                h2_ref, meta_ref, cnt_ref)


def _outproj_route(oa, ob, x, w_out_b, mod, nw2, rwhl, rwhi, rb, tril, seq, tm):
    t, d = x.shape
    tps = seq // tm
    out_specs, out_shapes = _tail_out(t, d, tm)
    return pl.pallas_call(
        _outproj_kernel,
        grid=(t // tm,),
        in_specs=[
            pl.BlockSpec((tm, WA), lambda i: (i, 0)),
            pl.BlockSpec((tm, WB), lambda i: (i, 0)),
            pl.BlockSpec((tm, d), lambda i: (i, 0)),
            pl.BlockSpec((WA + WB, d), lambda i: (0, 0)),
        ] + _tail_in_specs(d, tm, tps),
        out_specs=out_specs,
        out_shape=out_shapes,
        scratch_shapes=[pltpu.VMEM((1, LANES), F32)],
        compiler_params=_cparams(("arbitrary",)),
        name="outproj_route",
    )(oa, ob, x, w_out_b, mod, nw2, rwhl, rwhi, rb, tril)


def _pool_kernel(x_ref, nw1_ref, pw_ref, ps_ref, mod_ref, nw2_ref, rwhl_ref, rwhi_ref, rb_ref,
                 tril_ref, x1_ref, h2_ref, meta_ref, cnt_ref, carry_ref, e1_ref, e2_ref, e3_ref,
                 *, blocks_per_seq):
    tp, d = x_ref.shape
    hist = POOL_HIST
    i = pl.program_id(0)

    @pl.when(i == 0)
    def _():
        carry_ref[...] = jnp.zeros(carry_ref.shape, F32)

    @pl.when(i % blocks_per_seq == 0)
    def _():
        e1_ref[0:hist, :] = jnp.zeros((hist, d), F32)

    x = x_ref[...]
    h = _norm_mod(x, nw1_ref[...], mod_ref[0, 1:2, :], mod_ref[0, 0:1, :])
    e1_ref[hist:hist + tp, :] = h
    pos = (i % blocks_per_seq) * tp + lax.broadcasted_iota(I32, (tp, 1), 0)
    gdim = POOL_GROUP_DIM
    ys = []
    for gi, win in enumerate(POOL_WINDOWS):
        c0 = gi * gdim
        levels = win.bit_length() - 1
        src, spare = e1_ref, (e2_ref, e3_ref)
        for m in range(1, levels + 1):
            shift = 1 << (m - 1)
            start = hist - 8 * (levels - m)
            n = hist + tp - start
            cur = src[start:start + n, c0:c0 + gdim] + src[start - shift:start - shift + n, c0:c0 + gdim]
            if m < levels:
                dst = spare[m % 2]
                dst[start:start + n, c0:c0 + gdim] = cur
                src = dst
        cnt = jnp.minimum(pos + 1, win).astype(F32)
        pooled = cur / cnt - h[:, c0:c0 + gdim]
        ys.append(_mm(pooled, pw_ref[gi]))
    e1_ref[0:hist, :] = h[tp - hist:tp, :]
    y = jnp.concatenate(ys, axis=1) * ps_ref[...]
    x1 = x + mod_ref[0, 2:3, :] * y
    x1_ref[...] = x1
    _route_tail(x1, mod_ref, nw2_ref, rwhl_ref, rwhi_ref, rb_ref, tril_ref, carry_ref,
                h2_ref, meta_ref, cnt_ref)


def _pool_route(x, nw1, pool_w_b, pool_scale, mod, nw2, rwhl, rwhi, rb, tril, seq, tp):
    t, d = x.shape
    tps = seq // tp
    out_specs, out_shapes = _tail_out(t, d, tp)
    kern = functools.partial(_pool_kernel, blocks_per_seq=tps)
    return pl.pallas_call(
        kern,
        grid=(t // tp,),
        in_specs=[
            pl.BlockSpec((tp, d), lambda i: (i, 0)),
            pl.BlockSpec((1, d), lambda i: (0, 0)),
            pl.BlockSpec((4, POOL_GROUP_DIM, POOL_GROUP_DIM), lambda i: (0, 0, 0)),
            pl.BlockSpec((1, d), lambda i: (0, 0)),
        ] + _tail_in_specs(d, tp, tps),
        out_specs=out_specs,
        out_shape=out_shapes,
        scratch_shapes=[
            pltpu.VMEM((1, LANES), F32),
            pltpu.VMEM((tp + POOL_HIST, d), F32),
            pltpu.VMEM((tp + POOL_HIST, d), F32),
            pltpu.VMEM((tp + POOL_HIST, d), F32),
        ],
        compiler_params=_cparams(("arbitrary",)),
        name="pool_route",
    )(x, nw1, pool_w_b, pool_scale, mod, nw2, rwhl, rwhi, rb, tril)


def _dispatch_kernel(dest_ref, h_ref, zeros_ref, xs_ref, sem):
    del zeros_ref
    td = h_ref.shape[0]

    def issue(tk, carry):
        for s in range(2):
            d = dest_ref[0, 0, 2 * tk + s]
            pltpu.make_async_copy(h_ref.at[pl.ds(tk, 1), :], xs_ref.at[pl.ds(d, 1), :], sem).start()
        return carry

    lax.fori_loop(0, td, issue, 0)

    def drain(tk, carry):
        pltpu.make_async_copy(h_ref.at[pl.ds(0, 1), :], xs_ref.at[pl.ds(0, 1), :], sem).wait()
        return carry

    lax.fori_loop(0, 2 * td, drain, 0)


def _dispatch(dest3, h2, n_rows, td):
    t, d = h2.shape
    zeros = jnp.zeros((n_rows, d), h2.dtype)
    return pl.pallas_call(
        _dispatch_kernel,
        grid=(t // td,),
        in_specs=[
            pl.BlockSpec((1, 1, 2 * td), lambda i: (i, 0, 0), memory_space=pltpu.SMEM),
            pl.BlockSpec((td, d), lambda i: (i, 0)),
            pl.BlockSpec(memory_space=pl.ANY),
        ],
        out_specs=pl.BlockSpec(memory_space=pl.ANY),
        out_shape=jax.ShapeDtypeStruct((n_rows, d), h2.dtype),
        scratch_shapes=[pltpu.SemaphoreType.DMA(())],
        input_output_aliases={2: 0},
        compiler_params=_cparams(("arbitrary",)),
        name="moe_dispatch",
    )(dest3, h2, zeros)


def _expert_kernel(be_ref, nu_ref, xs_ref, wg_ref, wu_ref, wd_ref, y_ref, wgb_ref, wub_ref, wdb_ref):
    i = pl.program_id(0)
    prev = be_ref[jnp.maximum(i - 1, 0)]
    changed = (i == 0) | (be_ref[i] != prev)

    @pl.when(changed)
    def _():
        wgb_ref[...] = wg_ref[0].astype(BF16)
        wub_ref[...] = wu_ref[0].astype(BF16)
        wdb_ref[...] = wd_ref[0].astype(BF16)

    @pl.when(i < nu_ref[0])
    def _():
        xb = xs_ref[...].astype(BF16)
        g = jnp.dot(xb, wgb_ref[...], preferred_element_type=F32)
        u = jnp.dot(xb, wub_ref[...], preferred_element_type=F32)
        a = (_silu(g) * u).astype(BF16)
        y_ref[...] = jnp.dot(a, wdb_ref[...], preferred_element_type=F32)

    @pl.when(i >= nu_ref[0])
    def _():
        y_ref[...] = jnp.zeros(y_ref.shape, F32)


def _experts(blk_expert, n_used, xs, w_gate, w_up, w_down):
    n_rows, d = xs.shape
    n_blk = n_rows // MOE_ROWS
    return pl.pallas_call(
        _expert_kernel,
        grid_spec=pltpu.PrefetchScalarGridSpec(
            num_scalar_prefetch=2,
            grid=(n_blk,),
            in_specs=[
                pl.BlockSpec((MOE_ROWS, d), lambda i, be, nu: (i, 0)),
                pl.BlockSpec((1, d, D_FF), lambda i, be, nu: (be[i], 0, 0)),
                pl.BlockSpec((1, d, D_FF), lambda i, be, nu: (be[i], 0, 0)),
                pl.BlockSpec((1, D_FF, d), lambda i, be, nu: (be[i], 0, 0)),
            ],
            out_specs=pl.BlockSpec((MOE_ROWS, d), lambda i, be, nu: (i, 0)),
            scratch_shapes=[
                pltpu.VMEM((d, D_FF), BF16),
                pltpu.VMEM((d, D_FF), BF16),
                pltpu.VMEM((D_FF, d), BF16),
            ],
        ),
        out_shape=jax.ShapeDtypeStruct((n_rows, d), F32),
        compiler_params=_cparams(("arbitrary",)),
        name="moe_experts",
    )(blk_expert, n_used, xs, w_gate, w_up, w_down)


def _combine_kernel(dest_ref, ys_ref, x1_ref, meta_ref, mod_ref, fw_ref, o_ref, buf_ref, sem, *, final_norm):
    tc = x1_ref.shape[0]

    def issue(tk, carry):
        for s in range(2):
            d = dest_ref[0, 0, 2 * tk + s]
            pltpu.make_async_copy(ys_ref.at[pl.ds(d, 1), :], buf_ref.at[s, pl.ds(tk, 1), :], sem).start()
        return carry

    lax.fori_loop(0, tc, issue, 0)

    def drain(tk, carry):
        pltpu.make_async_copy(ys_ref.at[pl.ds(0, 1), :], buf_ref.at[0, pl.ds(0, 1), :], sem).wait()
        return carry

    lax.fori_loop(0, 2 * tc, drain, 0)

    meta = meta_ref[...]
    g1 = meta[:, M_G1:M_G1 + 1]
    g2 = meta[:, M_G2:M_G2 + 1]
    moe = buf_ref[0] * g1 + buf_ref[1] * g2
    x2 = x1_ref[...] + mod_ref[0, 5:6, :] * moe
    if final_norm:
        ms = jnp.mean(x2 * x2, axis=-1, keepdims=True)
        x2 = x2 * lax.rsqrt(ms + EPS) * fw_ref[...]
    o_ref[...] = x2


def _combine(dest3, ys, x1, meta, mod, fw, seq, tc, final_norm):
    t, d = x1.shape
    tps = seq // tc
    kern = functools.partial(_combine_kernel, final_norm=final_norm)
    return pl.pallas_call(
        kern,
        grid=(t // tc,),
        in_specs=[
            pl.BlockSpec((1, 1, 2 * tc), lambda i: (i, 0, 0), memory_space=pltpu.SMEM),
            pl.BlockSpec(memory_space=pl.ANY),
            pl.BlockSpec((tc, d), lambda i: (i, 0)),
            pl.BlockSpec((tc, LANES), lambda i: (i, 0)),
            pl.BlockSpec((1, 6, d), lambda i: (i // tps, 0, 0)),
            pl.BlockSpec((1, d), lambda i: (0, 0)),
        ],
        out_specs=pl.BlockSpec((tc, d), lambda i: (i, 0)),
        out_shape=jax.ShapeDtypeStruct((t, d), F32),
        scratch_shapes=[pltpu.VMEM((2, tc, d), F32), pltpu.SemaphoreType.DMA(())],
        compiler_params=_cparams(("arbitrary",)),
        name="moe_combine",
    )(dest3, ys, x1, meta, mod, fw)


def _moe(x1, h2, meta, cnt, mod, fw, w_gate, w_up, w_down, seq, tile, final_norm):
    t, d = x1.shape
    n_rows = -(-(2 * t + N_EXPERTS * (MOE_ROWS - 1)) // MOE_ROWS) * MOE_ROWS
    n_blk = n_rows // MOE_ROWS
    counts = cnt[0, :N_EXPERTS].astype(I32)
    padded = (counts + MOE_ROWS - 1) // MOE_ROWS * MOE_ROWS
    pad_end = jnp.cumsum(padded)
    pad_start = pad_end - padded
    expert = meta[:, M_E1:M_E2 + 1].astype(I32)
    rank = meta[:, M_R1:M_R2 + 1].astype(I32)
    dest = pad_start[expert] + rank
    dest3 = dest.reshape(t // tile, 1, 2 * tile)
    blk_expert = jnp.minimum(
        jnp.searchsorted(pad_end, jnp.arange(n_blk, dtype=I32) * MOE_ROWS, side='right'),
        N_EXPERTS - 1).astype(I32)
    n_used = (pad_end[-1:] // MOE_ROWS).astype(I32)
    xs = _dispatch(dest3, h2, n_rows, tile)
    ys = _experts(blk_expert, n_used, xs, w_gate, w_up, w_down)
    return _combine(dest3, ys, x1, meta, mod, fw, seq, tile, final_norm)


def _router_params(r1_w, r1_b, r2_w, r2_b):
    d = r1_w.shape[0]
    rw = jnp.zeros((d, LANES), F32).at[:, :N_EXPERTS].set(r2_w).at[:, R_GRP0:R_GRP0 + N_GROUPS].set(r1_w)
    rb = jnp.zeros((1, LANES), F32).at[0, :N_EXPERTS].set(r2_b).at[0, R_GRP0:R_GRP0 + N_GROUPS].set(r1_b)
    hi = rw.astype(BF16)
    lo = (rw - hi.astype(F32)).astype(BF16)
    return jnp.concatenate([hi, lo], axis=1), hi, rb


def kernel(x, c, ada_w, ada_b, norm_mix_w, norm_ffn_w, ab_w_in, ab_conv_w, ab_a_log, ab_dt_bias, ab_onorm_w, ab_sinks, rel_bias, ab_w_out, pool_w, pool_scale, r1_w, r1_b, r2_w, r2_b, moe_w_gate, moe_w_up, moe_w_down, final_norm_w):
    bsz, seq, d = x.shape
    t = bsz * seq
    tile = min(512, seq)
    lb = min(256, seq)
    xf = x.reshape(t, d)

    mod_all = _ada(c, ada_w, ada_b).reshape(DEPTH, bsz, 6, d)
    tril = jnp.tril(jnp.ones((tile, tile), BF16))
    fw = final_norm_w.reshape(1, d)

    w_in = ab_w_in[0]
    cuts = (0, WA, 2 * WA, 3 * WA, 4 * WA, 4 * WA + H_A, 4 * WA + 2 * H_A)
    qkvz = w_in[:, :cuts[4]]
    ba = w_in[:, cuts[4]:cuts[6]]
    qkv_b = w_in[:, cuts[6]:]
    w_all = jnp.concatenate([qkvz, qkv_b, ba, jnp.zeros((d, LANES - 2 * H_A), F32)], axis=1).astype(BF16)
    a1, b1, g1 = _inproj(xf, mod_all[0], norm_mix_w[0:1], w_all, seq, tile)

    pad = jnp.zeros((LANES - 2 * H_A,), F32)
    alog_row = jnp.concatenate([jnp.zeros((H_A,), F32), ab_a_log[0], pad]).reshape(1, LANES)
    dt_row = jnp.concatenate([jnp.zeros((H_A,), F32), ab_dt_bias[0], pad]).reshape(1, LANES)
    onw_row = jnp.concatenate([ab_onorm_w[0], ab_onorm_w[0]]).reshape(1, LANES)
    oa = _deltanet(a1, g1, ab_conv_w[0], alog_row, dt_row, onw_row, seq, lb)
    ob = _swa(b1, ab_sinks[0], _t5_bias_table(rel_bias), seq)

    rwhl, rwhi, rb = _router_params(r1_w[0], r1_b[0], r2_w[0], r2_b[0])
    x1, h2, meta, cnt = _outproj_route(oa, ob, xf, ab_w_out[0].astype(BF16), mod_all[0], norm_ffn_w[0:1],
                                       rwhl, rwhi, rb, tril, seq, tile)
    x2 = _moe(x1, h2, meta, cnt, mod_all[0], fw, moe_w_gate[0], moe_w_up[0], moe_w_down[0],
              seq, tile, False)

    rwhl, rwhi, rb = _router_params(r1_w[1], r1_b[1], r2_w[1], r2_b[1])
    x3, h2, meta, cnt = _pool_route(x2, norm_mix_w[1:2], pool_w[0].astype(BF16), pool_scale[0].reshape(1, d),
                                    mod_all[1], norm_ffn_w[1:2], rwhl, rwhi, rb, tril, seq, tile)
    out = _moe(x3, h2, meta, cnt, mod_all[1], fw, moe_w_gate[1], moe_w_up[1], moe_w_down[1],
               seq, tile, True)
    return out.reshape(bsz, seq, d)
```

```python
import functools
import math

import jax
import jax.numpy as jnp
from jax import lax
from jax.experimental import pallas as pl
from jax.experimental.pallas import tpu as pltpu

F32 = jnp.float32
BF16 = jnp.bfloat16
I32 = jnp.int32

D_MODEL = 1024
DEPTH = 2
H_A = 8
DK_A = 64
CONV_K = 4
CHUNK = 64
H_B = 8
H_B_KV = 2
DH_B = 64
WINDOW = 128
ATT_BLK = 128
NUM_BUCKETS = 32
MAX_DISTANCE = 128
POOL_WINDOWS = (2, 4, 8, 16)
POOL_GROUP_DIM = D_MODEL // 4
POOL_HIST = 32
N_GROUPS = 4
EXPERTS_PER_GROUP = 8
N_EXPERTS = N_GROUPS * EXPERTS_PER_GROUP
D_FF = D_MODEL // 2
EPS = 1e-6

WA = H_A * DK_A
WB = H_B * DH_B
WKV = H_B_KV * DH_B
N_PAIR = H_A // 2
LANES = 128
ROW_TILE = D_MODEL // LANES
MOE_ROWS = 256
VMEM_LIMIT = 56 * 1024 * 1024

M_E1, M_E2, M_R1, M_R2, M_G1, M_G2 = 0, 1, 2, 3, 4, 5
R_GRP0 = N_EXPERTS


def _cparams(sem):
    return pltpu.CompilerParams(dimension_semantics=sem, vmem_limit_bytes=VMEM_LIMIT)


def _mm(a, b):
    return jnp.dot(a.astype(BF16), b.astype(BF16), preferred_element_type=F32)


def _mm_nt(a, b):
    return lax.dot_general(a.astype(BF16), b.astype(BF16), (((1,), (1,)), ((), ())),
                           preferred_element_type=F32)


def _mm_x3(a, b):
    ah = a.astype(BF16)
    al = (a - ah.astype(F32)).astype(BF16)
    bh = b.astype(BF16)
    bl = (b - bh.astype(F32)).astype(BF16)
    return jnp.dot(jnp.concatenate([ah, al, ah], axis=1), jnp.concatenate([bh, bh, bl], axis=0),
                   preferred_element_type=F32)


def _split3(x):
    hi = x.astype(BF16)
    r = x - hi.astype(F32)
    mid = r.astype(BF16)
    lo = (r - mid.astype(F32)).astype(BF16)
    return hi, mid, lo


def _norm_mod(x, nw, sc, sh):
    ms = jnp.mean(x * x, axis=-1, keepdims=True)
    return (x * lax.rsqrt(ms + EPS) * nw) * (1.0 + sc) + sh


def _silu(x):
    return x * jax.nn.sigmoid(x)


def _ada_kernel(ct_ref, w_ref, b_ref, o_ref):
    nb = ct_ref.shape[1]
    d = ct_ref.shape[0]
    rows = []
    for b in range(nb):
        acc = None
        for k0 in range(0, d, LANES):
            ct = ct_ref[k0:k0 + LANES, b:b + 1]
            part = jnp.sum(_silu(ct) * w_ref[0, k0:k0 + LANES, :], axis=0, keepdims=True)
            acc = part if acc is None else acc + part
        rows.append(acc)
    o_ref[0] = jnp.concatenate(rows, axis=0) + b_ref[0]


def _ada(c, ada_w, ada_b):
    nb, d = c.shape
    depth, _, n6 = ada_w.shape
    cols = 1536
    return pl.pallas_call(
        _ada_kernel,
        grid=(depth, n6 // cols),
        in_specs=[
            pl.BlockSpec((d, nb), lambda l, j: (0, 0)),
            pl.BlockSpec((1, d, cols), lambda l, j: (l, 0, j)),
            pl.BlockSpec((1, 1, cols), lambda l, j: (l, 0, j)),
        ],
        out_specs=pl.BlockSpec((1, nb, cols), lambda l, j: (l, 0, j)),
        out_shape=jax.ShapeDtypeStruct((depth, nb, n6), F32),
        compiler_params=_cparams(("arbitrary", "arbitrary")),
        name="ada_mod",
    )(c.T, ada_w, ada_b.reshape(depth, 1, n6))


def _inproj_kernel(x_ref, mod_ref, nw_ref, w_ref, a_ref, b_ref, g_ref):
    h = _norm_mod(x_ref[...], nw_ref[...], mod_ref[0, 1:2, :], mod_ref[0, 0:1, :])
    hb = h.astype(BF16)
    na = a_ref.shape[1]
    nbb = b_ref.shape[1]
    a_ref[...] = jnp.dot(hb, w_ref[:, 0:na], preferred_element_type=F32)
    b_ref[...] = jnp.dot(hb, w_ref[:, na:na + nbb], preferred_element_type=F32).astype(BF16)
    g_ref[...] = jnp.dot(hb, w_ref[:, na + nbb:], preferred_element_type=F32)


def _inproj(x, mod, nw, w_all, seq, tm):
    t, d = x.shape
    tps = seq // tm
    na, nbb, ng = 4 * WA, WB + 2 * WKV, LANES
    return pl.pallas_call(
        _inproj_kernel,
        grid=(t // tm,),
        in_specs=[
            pl.BlockSpec((tm, d), lambda i: (i, 0)),
            pl.BlockSpec((1, 6, d), lambda i: (i // tps, 0, 0)),
            pl.BlockSpec((1, d), lambda i: (0, 0)),
            pl.BlockSpec((d, na + nbb + ng), lambda i: (0, 0)),
        ],
        out_specs=[
            pl.BlockSpec((tm, na), lambda i: (i, 0)),
            pl.BlockSpec((tm, nbb), lambda i: (i, 0)),
            pl.BlockSpec((tm, ng), lambda i: (i, 0)),
        ],
        out_shape=[
            jax.ShapeDtypeStruct((t, na), F32),
            jax.ShapeDtypeStruct((t, nbb), BF16),
            jax.ShapeDtypeStruct((t, ng), F32),
        ],
        compiler_params=_cparams(("arbitrary",)),
        name="inproj",
    )(x, mod, nw, w_all)


def _lane_half(shape):
    return lax.broadcasted_iota(I32, shape, 1) < (LANES // 2)


def _half_sums(x2, lo):
    se = jnp.sum(jnp.where(lo, x2, 0.0), axis=-1, keepdims=True)
    so = jnp.sum(jnp.where(lo, 0.0, x2), axis=-1, keepdims=True)
    return jnp.where(lo, se, so)


def _deltanet_kernel(a_ref, gp_ref, cw_ref, alog_ref, dt_ref, onw_ref, o_ref,
                     ext_ref, s_ref, q_ref, k_ref, v_ref, gf_ref, bf_ref, *, blocks_per_seq):
    lb = a_ref.shape[0]
    i = pl.program_id(0)

    @pl.when(i % blocks_per_seq == 0)
    def _():
        ext_ref[0:8, :] = jnp.zeros((8, ext_ref.shape[1]), F32)
        s_ref[...] = jnp.zeros(s_ref.shape, F32)

    lo_b = _lane_half((lb, LANES))
    for part, dst in ((0, q_ref), (1, k_ref), (2, v_ref)):
        for p in range(N_PAIR):
            c0 = part * WA + p * LANES
            xin = a_ref[:, c0:c0 + LANES]
            ext_ref[8:8 + lb, c0:c0 + LANES] = xin
            acc = cw_ref[CONV_K - 1:CONV_K, c0:c0 + LANES] * xin
            for j in range(CONV_K - 1):
                acc = acc + cw_ref[j:j + 1, c0:c0 + LANES] * ext_ref[pl.ds(8 - (CONV_K - 1) + j, lb), c0:c0 + LANES]
            ext_ref[0:8, c0:c0 + LANES] = xin[lb - 8:lb, :]
            y = _silu(acc)
            if part < 2:
                y = y * lax.rsqrt(_half_sums(y * y, lo_b) + EPS)
                if part == 0:
                    y = y * (DK_A ** -0.5)
            dst[:, p * LANES:(p + 1) * LANES] = y

    gp = gp_ref[...]
    bf_ref[...] = jax.nn.sigmoid(gp)
    xg = gp + dt_ref[...]
    softplus = jnp.maximum(xg, 0.0) + jnp.log(1.0 + jnp.exp(-jnp.abs(xg)))
    gf_ref[...] = -jnp.exp(alog_ref[...]) * softplus

    c = CHUNK
    c2 = 2 * c
    row = lax.broadcasted_iota(I32, (c2, c2), 0)
    col = lax.broadcasted_iota(I32, (c2, c2), 1)
    same_blk = (row < c) == (col < c)
    incl = same_blk & (row >= col)
    strict = same_blk & (row > col)
    eye = row == col
    eye_f = jnp.where(eye, 1.0, 0.0)
    lo = _lane_half((c, LANES))

    def stack(x):
        return jnp.where(same_blk, jnp.concatenate([x, x], axis=0), 0.0)

    def unstack(x):
        return jnp.where(lo, x[0:c], x[c:c2])

    rb = lax.broadcasted_iota(I32, (lb, lb), 0)
    cb = lax.broadcasted_iota(I32, (lb, lb), 1)
    shift = CHUNK.bit_length() - 1
    tril_chunks = jnp.where((jnp.right_shift(rb, shift) == jnp.right_shift(cb, shift)) & (rb >= cb),
                            1.0, 0.0).astype(BF16)
    g_hi, g_mid, g_lo = _split3(gf_ref[...])
    gf_ref[...] = (jnp.dot(tril_chunks, g_hi, preferred_element_type=F32)
                   + jnp.dot(tril_chunks, g_mid, preferred_element_type=F32)
                   + jnp.dot(tril_chunks, g_lo, preferred_element_type=F32))

    pairs = range(N_PAIR)

    def chunk_body(ci, carry):
        r0 = pl.multiple_of(ci * c, c)
        gcc = gf_ref[pl.ds(r0, c), :]
        bfc = bf_ref[pl.ds(r0, c), :]
        q = [q_ref[pl.ds(r0, c), p * LANES:(p + 1) * LANES] for p in pairs]
        k = [k_ref[pl.ds(r0, c), p * LANES:(p + 1) * LANES] for p in pairs]
        v = [v_ref[pl.ds(r0, c), p * LANES:(p + 1) * LANES] for p in pairs]

        def lane_bcast(x, j):
            return jnp.broadcast_to(x[:, j:j + 1], (c, LANES))

        beta = [jnp.where(lo, lane_bcast(bfc, 2 * p), lane_bcast(bfc, 2 * p + 1)) for p in pairs]
        gce = [lane_bcast(gcc, H_A + 2 * p) for p in pairs]
        gco = [lane_bcast(gcc, H_A + 2 * p + 1) for p in pairs]
        gcol = [jnp.concatenate([gce[p], gco[p]], axis=0) for p in pairs]
        gc = [jnp.where(lo, gce[p], gco[p]) for p in pairs]
        grow = [jnp.sum(jnp.where(eye, gcol[p], 0.0), axis=0, keepdims=True) for p in pairs]
        decay = [jnp.where(incl, jnp.exp(jnp.where(incl, gcol[p] - grow[p], 0.0)), 0.0) for p in pairs]
        egc = [jnp.exp(gc[p]) for p in pairs]
        kb = [k[p] * beta[p] for p in pairs]
        k_st = [stack(k[p]) for p in pairs]
        n_mat = [jnp.where(strict, _mm_nt(stack(kb[p]), k_st[p]) * decay[p], 0.0) for p in pairs]
        a_qk = [_mm_nt(stack(q[p]), k_st[p]) * decay[p] for p in pairs]
        x_inv = [eye_f - n_mat[p] for p in pairs]
        y_pow = [_mm_x3(n_mat[p], n_mat[p]) for p in pairs]
        for it in range(5):
            x_inv = [x_inv[p] + _mm_x3(x_inv[p], y_pow[p]) for p in pairs]
            if it < 4:
                y_pow = [_mm_x3(y_pow[p], y_pow[p]) for p in pairs]
        sol = []
        for p in pairs:
            vb = v[p] * beta[p]
            kbg = kb[p] * egc[p]
            rhs = jnp.concatenate([jnp.concatenate([vb, vb], axis=0),
                                   jnp.concatenate([kbg, kbg], axis=0)], axis=1)
            sol.append(_mm(x_inv[p], rhs))
        u = [unstack(sol[p][:, 0:LANES]) for p in pairs]
        w = [unstack(sol[p][:, LANES:2 * LANES]) for p in pairs]
        state = [s_ref[p] for p in pairs]
        wq = [_mm(jnp.concatenate([w[p], q[p] * egc[p]], axis=0), state[p]) for p in pairs]
        v_new = [u[p] - wq[p][0:c] for p in pairs]
        av = [_mm(a_qk[p], jnp.concatenate([v_new[p], v_new[p]], axis=0)) for p in pairs]
        for p in pairs:
            k_dec = k[p] * jnp.exp(gc[p][c - 1:c, :] - gc[p])
            g_tot = jnp.where(row < c, jnp.exp(gcol[p][c - 1:c, :]), jnp.exp(gcol[p][c2 - 1:c2, :]))
            s_ref[p] = state[p] * g_tot + jnp.where(same_blk, _mm(k_dec.T, v_new[p]), 0.0)
        for p in pairs:
            out = wq[p][c:c2] + unstack(av[p])
            ms = _half_sums(out * out, lo) * (1.0 / DK_A)
            z = a_ref[pl.ds(r0, c), 3 * WA + p * LANES:3 * WA + (p + 1) * LANES]
            res = out * lax.rsqrt(ms + EPS) * onw_ref[...] * _silu(z)
            o_ref[pl.ds(r0, c), p * LANES:(p + 1) * LANES] = res.astype(BF16)
        return carry

    lax.fori_loop(0, lb // c, chunk_body, 0)


def _deltanet(a1, g1, conv_w, alog_row, dt_row, onw_row, seq, lb):
    t = a1.shape[0]
    kern = functools.partial(_deltanet_kernel, blocks_per_seq=seq // lb)
    return pl.pallas_call(
        kern,
        grid=(t // lb,),
        in_specs=[
            pl.BlockSpec((lb, 4 * WA), lambda i: (i, 0)),
            pl.BlockSpec((lb, LANES), lambda i: (i, 0)),
            pl.BlockSpec((CONV_K, 3 * WA), lambda i: (0, 0)),
            pl.BlockSpec((1, LANES), lambda i: (0, 0)),
            pl.BlockSpec((1, LANES), lambda i: (0, 0)),
            pl.BlockSpec((1, LANES), lambda i: (0, 0)),
        ],
        out_specs=pl.BlockSpec((lb, WA), lambda i: (i, 0)),
        out_shape=jax.ShapeDtypeStruct((t, WA), BF16),
        scratch_shapes=[
            pltpu.VMEM((lb + 8, 3 * WA), F32),
            pltpu.VMEM((N_PAIR, LANES, LANES), F32),
            pltpu.VMEM((lb, WA), F32),
            pltpu.VMEM((lb, WA), F32),
            pltpu.VMEM((lb, WA), F32),
            pltpu.VMEM((lb, LANES), F32),
            pltpu.VMEM((lb, LANES), F32),
        ],
        compiler_params=_cparams(("arbitrary",)),
        name="deltanet",
    )(a1, g1, conv_w, alog_row, dt_row, onw_row)


def _swa_kernel(sink_ref, cur_ref, prev_ref, bias_ref, o_ref, *, blocks_per_seq):
    blk = ATT_BLK
    i = pl.program_id(0)
    first = (i % blocks_per_seq) == 0
    row = lax.broadcasted_iota(I32, (2 * blk, 2 * blk), 0)
    col = lax.broadcasted_iota(I32, (2 * blk, 2 * blk), 1)
    qi = jnp.where(row < blk, row, row - blk)
    dist = qi + blk - col
    first_key = jnp.where(first, blk, 0)
    valid = (dist >= 0) & (dist < WINDOW) & (col >= first_key)
    lo_kv = _lane_half((2 * blk, LANES))
    lo_q = _lane_half((blk, LANES))
    rowq = lax.broadcasted_iota(I32, (2 * blk, LANES), 0)
    q_keep = (rowq < blk) == lo_kv
    top_rows = lax.broadcasted_iota(I32, (2 * blk, 1), 0) < blk

    k0 = WB
    v0 = WB + WKV
    kk = jnp.concatenate([prev_ref[:, k0:k0 + WKV], cur_ref[:, k0:k0 + WKV]], axis=0).astype(F32)
    vv = jnp.concatenate([prev_ref[:, v0:v0 + WKV], cur_ref[:, v0:v0 + WKV]], axis=0).astype(F32)
    kk_sw = pltpu.roll(kk, LANES // 2, axis=1)
    vv_sw = pltpu.roll(vv, LANES // 2, axis=1)
    k_dup = (jnp.where(lo_kv, kk, kk_sw), jnp.where(lo_kv, kk_sw, kk))
    v_dup = (jnp.where(lo_kv, vv, vv_sw), jnp.where(lo_kv, vv_sw, vv))

    for p in range(H_B // 2):
        j = (2 * p) // (H_B // H_B_KV)
        qp = cur_ref[:, p * LANES:(p + 1) * LANES].astype(F32) * (DH_B ** -0.5)
        q_st = jnp.where(q_keep, jnp.concatenate([qp, qp], axis=0), 0.0)
        s = _mm_nt(q_st, k_dup[j])
        s = jnp.where(valid, s + bias_ref[p], -1e30)
        sink = jnp.where(top_rows, sink_ref[2 * p], sink_ref[2 * p + 1])
        m = jnp.maximum(jnp.max(s, axis=-1, keepdims=True), sink)
        pe = jnp.exp(s - m)
        den = jnp.sum(pe, axis=-1, keepdims=True) + jnp.exp(sink - m)
        o2 = _mm(pe, v_dup[j]) / den
        o_ref[:, p * LANES:(p + 1) * LANES] = jnp.where(lo_q, o2[0:blk], o2[blk:2 * blk]).astype(BF16)


def _swa(b1, sinks, bias_tab, seq):
    t, nc = b1.shape
    nbs = seq // ATT_BLK
    kern = functools.partial(_swa_kernel, blocks_per_seq=nbs)
    return pl.pallas_call(
        kern,
        grid=(t // ATT_BLK,),
        in_specs=[
            pl.BlockSpec(memory_space=pltpu.SMEM),
            pl.BlockSpec((ATT_BLK, nc), lambda i: (i, 0)),
            pl.BlockSpec((ATT_BLK, nc), lambda i: (jnp.maximum(i - 1, 0), 0)),
            pl.BlockSpec((H_B // 2, 2 * ATT_BLK, 2 * ATT_BLK), lambda i: (0, 0, 0)),
        ],
        out_specs=pl.BlockSpec((ATT_BLK, WB), lambda i: (i, 0)),
        out_shape=jax.ShapeDtypeStruct((t, WB), BF16),
        compiler_params=_cparams(("arbitrary",)),
        name="swa",
    )(sinks, b1, b1, bias_tab)


def _t5_bias_table(rel_bias):
    qi = jnp.arange(ATT_BLK)[:, None]
    ki = jnp.arange(2 * ATT_BLK)[None, :]
    n = jnp.maximum(qi + ATT_BLK - ki, 0)
    max_exact = NUM_BUCKETS // 2
    nf = jnp.maximum(n, 1).astype(F32)
    large = max_exact + (jnp.log(nf / max_exact) / math.log(MAX_DISTANCE / max_exact)
                         * (NUM_BUCKETS - max_exact)).astype(I32)
    bucket = jnp.where(n < max_exact, n, jnp.minimum(large, NUM_BUCKETS - 1))
    bias = rel_bias.astype(F32)[bucket]
    return bias.transpose(2, 0, 1).reshape(H_B // 2, 2 * ATT_BLK, 2 * ATT_BLK)


def _route_tail(x1, mod_ref, nw2_ref, rwhl_ref, rwhi_ref, rb_ref, tril_ref, carry_ref,
                h2_ref, meta_ref, metat_ref, cnt_ref):
    tm = x1.shape[0]
    h2 = _norm_mod(x1, nw2_ref[...], mod_ref[0, 4:5, :], mod_ref[0, 3:4, :])
    _store_row_tiles(h2_ref, h2)
    h_hi = h2.astype(BF16)
    h_lo = (h2 - h_hi.astype(F32)).astype(BF16)
    r = jnp.dot(h_hi, rwhl_ref[...], preferred_element_type=F32)
    logits = (r[:, 0:LANES] + r[:, LANES:2 * LANES]
              + jnp.dot(h_lo, rwhi_ref[...], preferred_element_type=F32) + rb_ref[...])
    lane = lax.broadcasted_iota(I32, (tm, LANES), 1)
    lane_f = lane.astype(F32)
    big = 1e9
    neg = -jnp.inf
    is_grp = (lane >= R_GRP0) & (lane < R_GRP0 + N_GROUPS)
    lg1 = jnp.where(is_grp, logits, neg)
    m1 = jnp.max(lg1, axis=-1, keepdims=True)
    grp = jnp.min(jnp.where(lg1 == m1, lane_f, big), axis=-1, keepdims=True) - R_GRP0
    p_top = 1.0 / jnp.sum(jnp.where(is_grp, jnp.exp(lg1 - m1), 0.0), axis=-1, keepdims=True)
    lane_grp = jnp.right_shift(lane, EXPERTS_PER_GROUP.bit_length() - 1).astype(F32)
    in_grp = (lane < N_EXPERTS) & (lane_grp == grp)
    l2 = jnp.where(in_grp, logits, neg)
    t1 = jnp.max(l2, axis=-1, keepdims=True)
    e1 = jnp.min(jnp.where(l2 == t1, lane_f, big), axis=-1, keepdims=True)
    l2b = jnp.where(lane_f == e1, neg, l2)
    t2 = jnp.max(l2b, axis=-1, keepdims=True)
    e2 = jnp.min(jnp.where(l2b == t2, lane_f, big), axis=-1, keepdims=True)
    ex = jnp.exp(t2 - t1)
    gate1 = p_top / (1.0 + ex)
    gate2 = p_top * ex / (1.0 + ex)
    hit1 = lane_f == e1
    hit2 = lane_f == e2
    onehot = jnp.where(hit1 | hit2, 1.0, 0.0)
    incl = jnp.dot(tril_ref[...], onehot.astype(BF16), preferred_element_type=F32)
    excl = incl - onehot + carry_ref[...]
    rank1 = jnp.sum(jnp.where(hit1, excl, 0.0), axis=-1, keepdims=True)
    rank2 = jnp.sum(jnp.where(hit2, excl, 0.0), axis=-1, keepdims=True)
    carry_ref[...] = carry_ref[...] + incl[tm - 1:tm, :]
    cnt_ref[...] = carry_ref[...]
    meta = jnp.zeros((tm, LANES), F32)
    for idx, val in ((M_E1, e1), (M_E2, e2), (M_R1, rank1), (M_R2, rank2), (M_G1, gate1), (M_G2, gate2)):
        meta = jnp.where(lane == idx, val, meta)
    meta_ref[...] = meta
    metat_ref[0] = meta.T[0:8, :]


def _chunk_rows(cidx, n):
    return pl.ds(cidx, n, stride=ROW_TILE)


def _store_row_tiles(ref, x):
    for cidx in range(ROW_TILE):
        ref[_chunk_rows(cidx, x.shape[0]), :] = x[:, cidx * LANES:(cidx + 1) * LANES]


def _load_row_tiles(ref):
    n = ref.shape[0] // ROW_TILE
    return jnp.concatenate([ref[_chunk_rows(cidx, n), :] for cidx in range(ROW_TILE)], axis=1)


def _tail_in_specs(d, tm, tps):
    return [
        pl.BlockSpec((1, 6, d), lambda i: (i // tps, 0, 0)),
        pl.BlockSpec((1, d), lambda i: (0, 0)),
        pl.BlockSpec((d, 2 * LANES), lambda i: (0, 0)),
        pl.BlockSpec((d, LANES), lambda i: (0, 0)),
        pl.BlockSpec((1, LANES), lambda i: (0, 0)),
        pl.BlockSpec((tm, tm), lambda i: (0, 0)),
    ]


def _tail_out(t, d, tm):
    specs = [
        pl.BlockSpec((tm, d), lambda i: (i, 0)),
        pl.BlockSpec((tm * ROW_TILE, LANES), lambda i: (i, 0)),
        pl.BlockSpec((tm, LANES), lambda i: (i, 0)),
        pl.BlockSpec((1, 8, tm), lambda i: (i, 0, 0)),
        pl.BlockSpec((1, LANES), lambda i: (0, 0)),
    ]
    shapes = [
        jax.ShapeDtypeStruct((t, d), F32),
        jax.ShapeDtypeStruct((t * ROW_TILE, LANES), F32),
        jax.ShapeDtypeStruct((t, LANES), F32),
        jax.ShapeDtypeStruct((t // tm, 8, tm), F32),
        jax.ShapeDtypeStruct((1, LANES), F32),
    ]
    return specs, shapes


def _outproj_kernel(oa_ref, ob_ref, x_ref, wo_ref, mod_ref, nw2_ref, rwhl_ref, rwhi_ref, rb_ref,
                    tril_ref, x1_ref, h2_ref, meta_ref, metat_ref, cnt_ref, carry_ref):
    @pl.when(pl.program_id(0) == 0)
    def _():
        carry_ref[...] = jnp.zeros(carry_ref.shape, F32)

    y = (jnp.dot(oa_ref[...], wo_ref[0:WA, :], preferred_element_type=F32)
         + jnp.dot(ob_ref[...], wo_ref[WA:WA + WB, :], preferred_element_type=F32))
    x1 = x_ref[...] + mod_ref[0, 2:3, :] * y
    x1_ref[...] = x1
    _route_tail(x1, mod_ref, nw2_ref, rwhl_ref, rwhi_ref, rb_ref, tril_ref, carry_ref,
                h2_ref, meta_ref, metat_ref, cnt_ref)


def _outproj_route(oa, ob, x, w_out_b, mod, nw2, rwhl, rwhi, rb, tril, seq, tm):
    t, d = x.shape
    tps = seq // tm
    out_specs, out_shapes = _tail_out(t, d, tm)
    return pl.pallas_call(
        _outproj_kernel,
        grid=(t // tm,),
        in_specs=[
            pl.BlockSpec((tm, WA), lambda i: (i, 0)),
            pl.BlockSpec((tm, WB), lambda i: (i, 0)),
            pl.BlockSpec((tm, d), lambda i: (i, 0)),
            pl.BlockSpec((WA + WB, d), lambda i: (0, 0)),
        ] + _tail_in_specs(d, tm, tps),
        out_specs=out_specs,
        out_shape=out_shapes,
        scratch_shapes=[pltpu.VMEM((1, LANES), F32)],
        compiler_params=_cparams(("arbitrary",)),
        name="outproj_route",
    )(oa, ob, x, w_out_b, mod, nw2, rwhl, rwhi, rb, tril)


def _pool_kernel(x_ref, nw1_ref, pw_ref, ps_ref, mod_ref, nw2_ref, rwhl_ref, rwhi_ref, rb_ref,
                 tril_ref, x1_ref, h2_ref, meta_ref, metat_ref, cnt_ref, carry_ref, e1_ref, e2_ref, e3_ref,
                 *, blocks_per_seq):
    tp, d = x_ref.shape
    hist = POOL_HIST
    i = pl.program_id(0)

    @pl.when(i == 0)
    def _():
        carry_ref[...] = jnp.zeros(carry_ref.shape, F32)

    @pl.when(i % blocks_per_seq == 0)
    def _():
        e1_ref[0:hist, :] = jnp.zeros((hist, d), F32)

    x = x_ref[...]
    h = _norm_mod(x, nw1_ref[...], mod_ref[0, 1:2, :], mod_ref[0, 0:1, :])
    e1_ref[hist:hist + tp, :] = h
    pos = (i % blocks_per_seq) * tp + lax.broadcasted_iota(I32, (tp, 1), 0)
    gdim = POOL_GROUP_DIM
    ys = []
    for gi, win in enumerate(POOL_WINDOWS):
        c0 = gi * gdim
        levels = win.bit_length() - 1
        src, spare = e1_ref, (e2_ref, e3_ref)
        for m in range(1, levels + 1):
            shift = 1 << (m - 1)
            start = hist - 8 * (levels - m)
            n = hist + tp - start
            cur = src[start:start + n, c0:c0 + gdim] + src[start - shift:start - shift + n, c0:c0 + gdim]
            if m < levels:
                dst = spare[m % 2]
                dst[start:start + n, c0:c0 + gdim] = cur
                src = dst
        cnt = jnp.minimum(pos + 1, win).astype(F32)
        pooled = cur / cnt - h[:, c0:c0 + gdim]
        ys.append(_mm(pooled, pw_ref[gi]))
    e1_ref[0:hist, :] = h[tp - hist:tp, :]
    y = jnp.concatenate(ys, axis=1) * ps_ref[...]
    x1 = x + mod_ref[0, 2:3, :] * y
    x1_ref[...] = x1
    _route_tail(x1, mod_ref, nw2_ref, rwhl_ref, rwhi_ref, rb_ref, tril_ref, carry_ref,
                h2_ref, meta_ref, metat_ref, cnt_ref)


def _pool_route(x, nw1, pool_w_b, pool_scale, mod, nw2, rwhl, rwhi, rb, tril, seq, tp):
    t, d = x.shape
    tps = seq // tp
    out_specs, out_shapes = _tail_out(t, d, tp)
    kern = functools.partial(_pool_kernel, blocks_per_seq=tps)
    return pl.pallas_call(
        kern,
        grid=(t // tp,),
        in_specs=[
            pl.BlockSpec((tp, d), lambda i: (i, 0)),
            pl.BlockSpec((1, d), lambda i: (0, 0)),
            pl.BlockSpec((4, POOL_GROUP_DIM, POOL_GROUP_DIM), lambda i: (0, 0, 0)),
            pl.BlockSpec((1, d), lambda i: (0, 0)),
        ] + _tail_in_specs(d, tp, tps),
        out_specs=out_specs,
        out_shape=out_shapes,
        scratch_shapes=[
            pltpu.VMEM((1, LANES), F32),
            pltpu.VMEM((tp + POOL_HIST, d), F32),
            pltpu.VMEM((tp + POOL_HIST, d), F32),
            pltpu.VMEM((tp + POOL_HIST, d), F32),
        ],
        compiler_params=_cparams(("arbitrary",)),
        name="pool_route",
    )(x, nw1, pool_w_b, pool_scale, mod, nw2, rwhl, rwhi, rb, tril)


def _dispatch_kernel(dest_ref, h_ref, zeros_ref, xs_ref, sem):
    del zeros_ref
    td = h_ref.shape[0] // ROW_TILE

    def row_tile(r):
        return pl.ds(pl.multiple_of(r * ROW_TILE, ROW_TILE), ROW_TILE)

    def issue(tk, carry):
        for s in range(2):
            d = dest_ref[0, 0, s * td + tk]
            pltpu.make_async_copy(h_ref.at[row_tile(tk)], xs_ref.at[row_tile(d)], sem).start()
        return carry

    lax.fori_loop(0, td, issue, 0)

    def drain(tk, carry):
        pltpu.make_async_copy(h_ref.at[row_tile(0)], xs_ref.at[row_tile(0)], sem).wait()
        return carry

    lax.fori_loop(0, 2 * td, drain, 0)


def _dispatch(dest3, h2, n_rows, td):
    t = h2.shape[0] // ROW_TILE
    zeros = jnp.zeros((n_rows * ROW_TILE, LANES), h2.dtype)
    return pl.pallas_call(
        _dispatch_kernel,
        grid=(t // td,),
        in_specs=[
            pl.BlockSpec((1, 1, 2 * td), lambda i: (i, 0, 0), memory_space=pltpu.SMEM),
            pl.BlockSpec((td * ROW_TILE, LANES), lambda i: (i, 0)),
            pl.BlockSpec(memory_space=pl.ANY),
        ],
        out_specs=pl.BlockSpec(memory_space=pl.ANY),
        out_shape=jax.ShapeDtypeStruct((n_rows * ROW_TILE, LANES), h2.dtype),
        scratch_shapes=[pltpu.SemaphoreType.DMA(())],
        input_output_aliases={2: 0},
        compiler_params=_cparams(("arbitrary",)),
        name="moe_dispatch",
    )(dest3, h2, zeros)


def _expert_kernel(be_ref, nu_ref, xs_ref, wg_ref, wu_ref, wd_ref, y_ref, wgb_ref, wub_ref, wdb_ref):
    i = pl.program_id(0)
    prev = be_ref[jnp.maximum(i - 1, 0)]
    changed = (i == 0) | (be_ref[i] != prev)

    @pl.when(changed)
    def _():
        wgb_ref[...] = wg_ref[0].astype(BF16)
        wub_ref[...] = wu_ref[0].astype(BF16)
        wdb_ref[...] = wd_ref[0].astype(BF16)

    @pl.when(i < nu_ref[0])
    def _():
        xb = _load_row_tiles(xs_ref).astype(BF16)
        g = jnp.dot(xb, wgb_ref[...], preferred_element_type=F32)
        u = jnp.dot(xb, wub_ref[...], preferred_element_type=F32)
        a = (_silu(g) * u).astype(BF16)
        _store_row_tiles(y_ref, jnp.dot(a, wdb_ref[...], preferred_element_type=F32))

    @pl.when(i >= nu_ref[0])
    def _():
        y_ref[...] = jnp.zeros(y_ref.shape, F32)


def _experts(blk_expert, n_used, xs, w_gate, w_up, w_down):
    n_rows = xs.shape[0] // ROW_TILE
    d = ROW_TILE * LANES
    blk = (MOE_ROWS * ROW_TILE, LANES)
    n_blk = n_rows // MOE_ROWS
    return pl.pallas_call(
        _expert_kernel,
        grid_spec=pltpu.PrefetchScalarGridSpec(
            num_scalar_prefetch=2,
            grid=(n_blk,),
            in_specs=[
                pl.BlockSpec(blk, lambda i, be, nu: (i, 0)),
                pl.BlockSpec((1, d, D_FF), lambda i, be, nu: (be[i], 0, 0)),
                pl.BlockSpec((1, d, D_FF), lambda i, be, nu: (be[i], 0, 0)),
                pl.BlockSpec((1, D_FF, d), lambda i, be, nu: (be[i], 0, 0)),
            ],
            out_specs=pl.BlockSpec(blk, lambda i, be, nu: (i, 0)),
            scratch_shapes=[
                pltpu.VMEM((d, D_FF), BF16),
                pltpu.VMEM((d, D_FF), BF16),
                pltpu.VMEM((D_FF, d), BF16),
            ],
        ),
        out_shape=jax.ShapeDtypeStruct(xs.shape, F32),
        compiler_params=_cparams(("arbitrary",)),
        name="moe_experts",
    )(blk_expert, n_used, xs, w_gate, w_up, w_down)


def _combine_kernel(dest_ref, ys_ref, x1_ref, meta_ref, mod_ref, fw_ref, o_ref, buf_ref, sem, *, final_norm):
    tc = x1_ref.shape[0]

    def row_tile(r):
        return pl.ds(pl.multiple_of(r * ROW_TILE, ROW_TILE), ROW_TILE)

    def issue(tk, carry):
        for s in range(2):
            d = dest_ref[0, 0, s * tc + tk]
            pltpu.make_async_copy(ys_ref.at[row_tile(d)], buf_ref.at[s, row_tile(tk)], sem).start()
        return carry

    lax.fori_loop(0, tc, issue, 0)

    def drain(tk, carry):
        pltpu.make_async_copy(ys_ref.at[row_tile(0)], buf_ref.at[0, row_tile(0)], sem).wait()
        return carry

    lax.fori_loop(0, 2 * tc, drain, 0)

    meta = meta_ref[...]
    g1 = meta[:, M_G1:M_G1 + 1]
    g2 = meta[:, M_G2:M_G2 + 1]
    n_chunks = ROW_TILE
    ssq = jnp.zeros((tc, 1), F32)
    for cidx in range(n_chunks):
        sl = slice(cidx * LANES, (cidx + 1) * LANES)
        rows = _chunk_rows(cidx, tc)
        moe = buf_ref[0, rows, :] * g1 + buf_ref[1, rows, :] * g2
        x2 = x1_ref[:, sl] + mod_ref[0, 5:6, sl] * moe
        o_ref[:, sl] = x2
        if final_norm:
            ssq = ssq + jnp.sum(x2 * x2, axis=-1, keepdims=True)
    if final_norm:
        inv = lax.rsqrt(ssq * (1.0 / (n_chunks * LANES)) + EPS)
        for cidx in range(n_chunks):
            sl = slice(cidx * LANES, (cidx + 1) * LANES)
            o_ref[:, sl] = o_ref[:, sl] * inv * fw_ref[:, sl]


def _combine(dest3, ys, x1, meta, mod, fw, seq, tc, final_norm):
    t, d = x1.shape
    tps = seq // tc
    kern = functools.partial(_combine_kernel, final_norm=final_norm)
    return pl.pallas_call(
        kern,
        grid=(t // tc,),
        in_specs=[
            pl.BlockSpec((1, 1, 2 * tc), lambda i: (i, 0, 0), memory_space=pltpu.SMEM),
            pl.BlockSpec(memory_space=pl.ANY),
            pl.BlockSpec((tc, d), lambda i: (i, 0)),
            pl.BlockSpec((tc, LANES), lambda i: (i, 0)),
            pl.BlockSpec((1, 6, d), lambda i: (i // tps, 0, 0)),
            pl.BlockSpec((1, d), lambda i: (0, 0)),
        ],
        out_specs=pl.BlockSpec((tc, d), lambda i: (i, 0)),
        out_shape=jax.ShapeDtypeStruct((t, d), F32),
        scratch_shapes=[pltpu.VMEM((2, tc * ROW_TILE, LANES), F32), pltpu.SemaphoreType.DMA(())],
        compiler_params=_cparams(("arbitrary",)),
        name="moe_combine",
    )(dest3, ys, x1, meta, mod, fw)


def _moe(x1, h2, meta, metat, cnt, mod, fw, w_gate, w_up, w_down, seq, tile, final_norm):
    t, d = x1.shape
    n_rows = -(-(2 * t + N_EXPERTS * (MOE_ROWS - 1)) // MOE_ROWS) * MOE_ROWS
    n_blk = n_rows // MOE_ROWS
    counts = cnt[0, :N_EXPERTS].astype(I32)
    padded = (counts + MOE_ROWS - 1) // MOE_ROWS * MOE_ROWS
    pad_end = jnp.cumsum(padded)
    pad_start = pad_end - padded
    expert = metat[:, M_E1:M_E2 + 1, :].astype(I32)
    rank = metat[:, M_R1:M_R2 + 1, :].astype(I32)
    start_of = jnp.sum(jnp.where(expert[..., None] == jnp.arange(N_EXPERTS, dtype=I32), pad_start, 0), axis=-1)
    dest3 = (start_of + rank).reshape(t // tile, 1, 2 * tile)
    blk_row0 = jnp.arange(n_blk, dtype=I32) * MOE_ROWS
    blk_expert = jnp.minimum(jnp.sum((pad_end[None, :] <= blk_row0[:, None]).astype(I32), axis=1),
                             N_EXPERTS - 1)
    n_used = (pad_end[-1:] // MOE_ROWS).astype(I32)
    xs = _dispatch(dest3, h2, n_rows, tile)
    ys = _experts(blk_expert, n_used, xs, w_gate, w_up, w_down)
    return _combine(dest3, ys, x1, meta, mod, fw, seq, tile, final_norm)


def _router_params(r1_w, r1_b, r2_w, r2_b):
    d = r1_w.shape[0]
    rw = jnp.zeros((d, LANES), F32).at[:, :N_EXPERTS].set(r2_w).at[:, R_GRP0:R_GRP0 + N_GROUPS].set(r1_w)
    rb = jnp.zeros((1, LANES), F32).at[0, :N_EXPERTS].set(r2_b).at[0, R_GRP0:R_GRP0 + N_GROUPS].set(r1_b)
    hi = rw.astype(BF16)
    lo = (rw - hi.astype(F32)).astype(BF16)
    return jnp.concatenate([hi, lo], axis=1), hi, rb


def kernel(x, c, ada_w, ada_b, norm_mix_w, norm_ffn_w, ab_w_in, ab_conv_w, ab_a_log, ab_dt_bias, ab_onorm_w, ab_sinks, rel_bias, ab_w_out, pool_w, pool_scale, r1_w, r1_b, r2_w, r2_b, moe_w_gate, moe_w_up, moe_w_down, final_norm_w):
    bsz, seq, d = x.shape
    t = bsz * seq
    tile = min(512, seq)
    lb = min(256, seq)
    xf = x.reshape(t, d)

    mod_all = _ada(c, ada_w, ada_b).reshape(DEPTH, bsz, 6, d)
    tril = jnp.tril(jnp.ones((tile, tile), BF16))
    fw = final_norm_w.reshape(1, d)

    w_in = ab_w_in[0]
    cuts = (0, WA, 2 * WA, 3 * WA, 4 * WA, 4 * WA + H_A, 4 * WA + 2 * H_A)
    qkvz = w_in[:, :cuts[4]]
    ba = w_in[:, cuts[4]:cuts[6]]
    qkv_b = w_in[:, cuts[6]:]
    w_all = jnp.concatenate([qkvz, qkv_b, ba, jnp.zeros((d, LANES - 2 * H_A), F32)], axis=1).astype(BF16)
    a1, b1, g1 = _inproj(xf, mod_all[0], norm_mix_w[0:1], w_all, seq, tile)

    pad = jnp.zeros((LANES - 2 * H_A,), F32)
    alog_row = jnp.concatenate([jnp.zeros((H_A,), F32), ab_a_log[0], pad]).reshape(1, LANES)
    dt_row = jnp.concatenate([jnp.zeros((H_A,), F32), ab_dt_bias[0], pad]).reshape(1, LANES)
    onw_row = jnp.concatenate([ab_onorm_w[0], ab_onorm_w[0]]).reshape(1, LANES)
    oa = _deltanet(a1, g1, ab_conv_w[0], alog_row, dt_row, onw_row, seq, lb)
    ob = _swa(b1, ab_sinks[0], _t5_bias_table(rel_bias), seq)

    rwhl, rwhi, rb = _router_params(r1_w[0], r1_b[0], r2_w[0], r2_b[0])
    x1, h2, meta, metat, cnt = _outproj_route(oa, ob, xf, ab_w_out[0].astype(BF16), mod_all[0], norm_ffn_w[0:1],
                                       rwhl, rwhi, rb, tril, seq, tile)
    x2 = _moe(x1, h2, meta, metat, cnt, mod_all[0], fw, moe_w_gate[0], moe_w_up[0], moe_w_down[0],
              seq, tile, False)

    rwhl, rwhi, rb = _router_params(r1_w[1], r1_b[1], r2_w[1], r2_b[1])
    x3, h2, meta, metat, cnt = _pool_route(x2, norm_mix_w[1:2], pool_w[0].astype(BF16), pool_scale[0].reshape(1, d),
                                    mod_all[1], norm_ffn_w[1:2], rwhl, rwhi, rb, tril, seq, tile)
    out = _moe(x3, h2, meta, metat, cnt, mod_all[1], fw, moe_w_gate[1], moe_w_up[1], moe_w_down[1],
               seq, tile, True)
    return out.reshape(bsz, seq, d)
```

```python
import functools
import math

import jax
import jax.numpy as jnp
from jax import lax
from jax.experimental import pallas as pl
from jax.experimental.pallas import tpu as pltpu

F32 = jnp.float32
BF16 = jnp.bfloat16
I32 = jnp.int32

D_MODEL = 1024
DEPTH = 2
H_A = 8
DK_A = 64
CONV_K = 4
CHUNK = 64
H_B = 8
H_B_KV = 2
DH_B = 64
WINDOW = 128
ATT_BLK = 128
NUM_BUCKETS = 32
MAX_DISTANCE = 128
POOL_WINDOWS = (2, 4, 8, 16)
POOL_GROUP_DIM = D_MODEL // 4
POOL_HIST = 32
N_GROUPS = 4
EXPERTS_PER_GROUP = 8
N_EXPERTS = N_GROUPS * EXPERTS_PER_GROUP
D_FF = D_MODEL // 2
EPS = 1e-6

WA = H_A * DK_A
WB = H_B * DH_B
WKV = H_B_KV * DH_B
N_PAIR = H_A // 2
LANES = 128
ROW_TILE = D_MODEL // LANES
MOE_ROWS = 256
VMEM_LIMIT = 56 * 1024 * 1024

M_E1, M_E2, M_R1, M_R2, M_G1, M_G2 = 0, 1, 2, 3, 4, 5
R_GRP0 = N_EXPERTS


def _cparams(sem):
    return pltpu.CompilerParams(dimension_semantics=sem, vmem_limit_bytes=VMEM_LIMIT)


def _mm(a, b):
    return jnp.dot(a.astype(BF16), b.astype(BF16), preferred_element_type=F32)


def _mm_nt(a, b):
    return lax.dot_general(a.astype(BF16), b.astype(BF16), (((1,), (1,)), ((), ())),
                           preferred_element_type=F32)


def _mm_x3(a, b):
    ah = a.astype(BF16)
    al = (a - ah.astype(F32)).astype(BF16)
    bh = b.astype(BF16)
    bl = (b - bh.astype(F32)).astype(BF16)
    return jnp.dot(jnp.concatenate([ah, al, ah], axis=1), jnp.concatenate([bh, bh, bl], axis=0),
                   preferred_element_type=F32)


def _split3(x):
    hi = x.astype(BF16)
    r = x - hi.astype(F32)
    mid = r.astype(BF16)
    lo = (r - mid.astype(F32)).astype(BF16)
    return hi, mid, lo


def _norm_mod(x, nw, sc, sh):
    ms = jnp.mean(x * x, axis=-1, keepdims=True)
    return (x * lax.rsqrt(ms + EPS) * nw) * (1.0 + sc) + sh


def _silu(x):
    return x * jax.nn.sigmoid(x)


def _ada_kernel(ct_ref, w_ref, b_ref, o_ref):
    nb = ct_ref.shape[1]
    d = ct_ref.shape[0]
    rows = []
    for b in range(nb):
        acc = None
        for k0 in range(0, d, LANES):
            ct = ct_ref[k0:k0 + LANES, b:b + 1]
            part = jnp.sum(_silu(ct) * w_ref[0, k0:k0 + LANES, :], axis=0, keepdims=True)
            acc = part if acc is None else acc + part
        rows.append(acc)
    o_ref[0] = jnp.concatenate(rows, axis=0) + b_ref[0]


def _ada(c, ada_w, ada_b):
    nb, d = c.shape
    depth, _, n6 = ada_w.shape
    cols = 1536
    return pl.pallas_call(
        _ada_kernel,
        grid=(depth, n6 // cols),
        in_specs=[
            pl.BlockSpec((d, nb), lambda l, j: (0, 0)),
            pl.BlockSpec((1, d, cols), lambda l, j: (l, 0, j)),
            pl.BlockSpec((1, 1, cols), lambda l, j: (l, 0, j)),
        ],
        out_specs=pl.BlockSpec((1, nb, cols), lambda l, j: (l, 0, j)),
        out_shape=jax.ShapeDtypeStruct((depth, nb, n6), F32),
        compiler_params=_cparams(("arbitrary", "arbitrary")),
        name="ada_mod",
    )(c.T, ada_w, ada_b.reshape(depth, 1, n6))


def _inproj_kernel(x_ref, mod_ref, nw_ref, w_ref, a_ref, b_ref, g_ref):
    h = _norm_mod(x_ref[...], nw_ref[...], mod_ref[0, 1:2, :], mod_ref[0, 0:1, :])
    hb = h.astype(BF16)
    na = a_ref.shape[1]
    nbb = b_ref.shape[1]
    a_ref[...] = jnp.dot(hb, w_ref[:, 0:na], preferred_element_type=F32)
    b_ref[...] = jnp.dot(hb, w_ref[:, na:na + nbb], preferred_element_type=F32).astype(BF16)
    g_ref[...] = jnp.dot(hb, w_ref[:, na + nbb:], preferred_element_type=F32)


def _inproj(x, mod, nw, w_all, seq, tm):
    t, d = x.shape
    tps = seq // tm
    na, nbb, ng = 4 * WA, WB + 2 * WKV, LANES
    return pl.pallas_call(
        _inproj_kernel,
        grid=(t // tm,),
        in_specs=[
            pl.BlockSpec((tm, d), lambda i: (i, 0)),
            pl.BlockSpec((1, 6, d), lambda i: (i // tps, 0, 0)),
            pl.BlockSpec((1, d), lambda i: (0, 0)),
            pl.BlockSpec((d, na + nbb + ng), lambda i: (0, 0)),
        ],
        out_specs=[
            pl.BlockSpec((tm, na), lambda i: (i, 0)),
            pl.BlockSpec((tm, nbb), lambda i: (i, 0)),
            pl.BlockSpec((tm, ng), lambda i: (i, 0)),
        ],
        out_shape=[
            jax.ShapeDtypeStruct((t, na), F32),
            jax.ShapeDtypeStruct((t, nbb), BF16),
            jax.ShapeDtypeStruct((t, ng), F32),
        ],
        compiler_params=_cparams(("arbitrary",)),
        name="inproj",
    )(x, mod, nw, w_all)


def _lane_half(shape):
    return lax.broadcasted_iota(I32, shape, 1) < (LANES // 2)


def _half_sums(x2, lo):
    se = jnp.sum(jnp.where(lo, x2, 0.0), axis=-1, keepdims=True)
    so = jnp.sum(jnp.where(lo, 0.0, x2), axis=-1, keepdims=True)
    return jnp.where(lo, se, so)


def _deltanet_kernel(a_ref, gp_ref, cw_ref, alog_ref, dt_ref, onw_ref, o_ref,
                     ext_ref, s_ref, q_ref, k_ref, v_ref, gf_ref, bf_ref, *, blocks_per_seq):
    lb = a_ref.shape[0]
    i = pl.program_id(0)

    @pl.when(i % blocks_per_seq == 0)
    def _():
        ext_ref[0:8, :] = jnp.zeros((8, ext_ref.shape[1]), F32)
        s_ref[...] = jnp.zeros(s_ref.shape, F32)

    lo_b = _lane_half((lb, LANES))
    for part, dst in ((0, q_ref), (1, k_ref), (2, v_ref)):
        for p in range(N_PAIR):
            c0 = part * WA + p * LANES
            xin = a_ref[:, c0:c0 + LANES]
            ext_ref[8:8 + lb, c0:c0 + LANES] = xin
            acc = cw_ref[CONV_K - 1:CONV_K, c0:c0 + LANES] * xin
            for j in range(CONV_K - 1):
                acc = acc + cw_ref[j:j + 1, c0:c0 + LANES] * ext_ref[pl.ds(8 - (CONV_K - 1) + j, lb), c0:c0 + LANES]
            ext_ref[0:8, c0:c0 + LANES] = xin[lb - 8:lb, :]
            y = _silu(acc)
            if part < 2:
                y = y * lax.rsqrt(_half_sums(y * y, lo_b) + EPS)
                if part == 0:
                    y = y * (DK_A ** -0.5)
            dst[:, p * LANES:(p + 1) * LANES] = y

    gp = gp_ref[...]
    bf_ref[...] = jax.nn.sigmoid(gp)
    xg = gp + dt_ref[...]
    softplus = jnp.maximum(xg, 0.0) + jnp.log(1.0 + jnp.exp(-jnp.abs(xg)))
    gf_ref[...] = -jnp.exp(alog_ref[...]) * softplus

    c = CHUNK
    c2 = 2 * c
    row = lax.broadcasted_iota(I32, (c2, c2), 0)
    col = lax.broadcasted_iota(I32, (c2, c2), 1)
    same_blk = (row < c) == (col < c)
    incl = same_blk & (row >= col)
    strict = same_blk & (row > col)
    eye = row == col
    eye_f = jnp.where(eye, 1.0, 0.0)
    lo = _lane_half((c, LANES))

    def stack(x):
        return jnp.where(same_blk, jnp.concatenate([x, x], axis=0), 0.0)

    def unstack(x):
        return jnp.where(lo, x[0:c], x[c:c2])

    rb = lax.broadcasted_iota(I32, (lb, lb), 0)
    cb = lax.broadcasted_iota(I32, (lb, lb), 1)
    shift = CHUNK.bit_length() - 1
    tril_chunks = jnp.where((jnp.right_shift(rb, shift) == jnp.right_shift(cb, shift)) & (rb >= cb),
                            1.0, 0.0).astype(BF16)
    g_hi, g_mid, g_lo = _split3(gf_ref[...])
    gf_ref[...] = (jnp.dot(tril_chunks, g_hi, preferred_element_type=F32)
                   + jnp.dot(tril_chunks, g_mid, preferred_element_type=F32)
                   + jnp.dot(tril_chunks, g_lo, preferred_element_type=F32))

    pairs = range(N_PAIR)

    def chunk_body(ci, carry):
        r0 = pl.multiple_of(ci * c, c)
        gcc = gf_ref[pl.ds(r0, c), :]
        bfc = bf_ref[pl.ds(r0, c), :]
        q = [q_ref[pl.ds(r0, c), p * LANES:(p + 1) * LANES] for p in pairs]
        k = [k_ref[pl.ds(r0, c), p * LANES:(p + 1) * LANES] for p in pairs]
        v = [v_ref[pl.ds(r0, c), p * LANES:(p + 1) * LANES] for p in pairs]

        def lane_bcast(x, j):
            return jnp.broadcast_to(x[:, j:j + 1], (c, LANES))

        beta = [jnp.where(lo, lane_bcast(bfc, 2 * p), lane_bcast(bfc, 2 * p + 1)) for p in pairs]
        gce = [lane_bcast(gcc, H_A + 2 * p) for p in pairs]
        gco = [lane_bcast(gcc, H_A + 2 * p + 1) for p in pairs]
        gcol = [jnp.concatenate([gce[p], gco[p]], axis=0) for p in pairs]
        gc = [jnp.where(lo, gce[p], gco[p]) for p in pairs]
        grow = [jnp.sum(jnp.where(eye, gcol[p], 0.0), axis=0, keepdims=True) for p in pairs]
        decay = [jnp.where(incl, jnp.exp(jnp.where(incl, gcol[p] - grow[p], 0.0)), 0.0) for p in pairs]
        egc = [jnp.exp(gc[p]) for p in pairs]
        kb = [k[p] * beta[p] for p in pairs]
        k_st = [stack(k[p]) for p in pairs]
        n_mat = [jnp.where(strict, _mm_nt(stack(kb[p]), k_st[p]) * decay[p], 0.0) for p in pairs]
        a_qk = [_mm_nt(stack(q[p]), k_st[p]) * decay[p] for p in pairs]
        x_inv = [eye_f - n_mat[p] for p in pairs]
        y_pow = [_mm_x3(n_mat[p], n_mat[p]) for p in pairs]
        for it in range(5):
            x_inv = [x_inv[p] + _mm_x3(x_inv[p], y_pow[p]) for p in pairs]
            if it < 4:
                y_pow = [_mm_x3(y_pow[p], y_pow[p]) for p in pairs]
        sol = []
        for p in pairs:
            vb = v[p] * beta[p]
            kbg = kb[p] * egc[p]
            rhs = jnp.concatenate([jnp.concatenate([vb, vb], axis=0),
                                   jnp.concatenate([kbg, kbg], axis=0)], axis=1)
            sol.append(_mm(x_inv[p], rhs))
        u = [unstack(sol[p][:, 0:LANES]) for p in pairs]
        w = [unstack(sol[p][:, LANES:2 * LANES]) for p in pairs]
        state = [s_ref[p] for p in pairs]
        wq = [_mm(jnp.concatenate([w[p], q[p] * egc[p]], axis=0), state[p]) for p in pairs]
        v_new = [u[p] - wq[p][0:c] for p in pairs]
        av = [_mm(a_qk[p], jnp.concatenate([v_new[p], v_new[p]], axis=0)) for p in pairs]
        for p in pairs:
            k_dec = k[p] * jnp.exp(gc[p][c - 1:c, :] - gc[p])
            g_tot = jnp.where(row < c, jnp.exp(gcol[p][c - 1:c, :]), jnp.exp(gcol[p][c2 - 1:c2, :]))
            s_ref[p] = state[p] * g_tot + jnp.where(same_blk, _mm(k_dec.T, v_new[p]), 0.0)
        for p in pairs:
            out = wq[p][c:c2] + unstack(av[p])
            ms = _half_sums(out * out, lo) * (1.0 / DK_A)
            z = a_ref[pl.ds(r0, c), 3 * WA + p * LANES:3 * WA + (p + 1) * LANES]
            res = out * lax.rsqrt(ms + EPS) * onw_ref[...] * _silu(z)
            o_ref[pl.ds(r0, c), p * LANES:(p + 1) * LANES] = res.astype(BF16)
        return carry

    lax.fori_loop(0, lb // c, chunk_body, 0)


def _deltanet(a1, g1, conv_w, alog_row, dt_row, onw_row, seq, lb):
    t = a1.shape[0]
    kern = functools.partial(_deltanet_kernel, blocks_per_seq=seq // lb)
    return pl.pallas_call(
        kern,
        grid=(t // lb,),
        in_specs=[
            pl.BlockSpec((lb, 4 * WA), lambda i: (i, 0)),
            pl.BlockSpec((lb, LANES), lambda i: (i, 0)),
            pl.BlockSpec((CONV_K, 3 * WA), lambda i: (0, 0)),
            pl.BlockSpec((1, LANES), lambda i: (0, 0)),
            pl.BlockSpec((1, LANES), lambda i: (0, 0)),
            pl.BlockSpec((1, LANES), lambda i: (0, 0)),
        ],
        out_specs=pl.BlockSpec((lb, WA), lambda i: (i, 0)),
        out_shape=jax.ShapeDtypeStruct((t, WA), BF16),
        scratch_shapes=[
            pltpu.VMEM((lb + 8, 3 * WA), F32),
            pltpu.VMEM((N_PAIR, LANES, LANES), F32),
            pltpu.VMEM((lb, WA), F32),
            pltpu.VMEM((lb, WA), F32),
            pltpu.VMEM((lb, WA), F32),
            pltpu.VMEM((lb, LANES), F32),
            pltpu.VMEM((lb, LANES), F32),
        ],
        compiler_params=_cparams(("arbitrary",)),
        name="deltanet",
    )(a1, g1, conv_w, alog_row, dt_row, onw_row)


def _swa_kernel(sink_ref, cur_ref, prev_ref, bias_ref, o_ref, *, blocks_per_seq):
    blk = ATT_BLK
    i = pl.program_id(0)
    first = (i % blocks_per_seq) == 0
    row = lax.broadcasted_iota(I32, (2 * blk, 2 * blk), 0)
    col = lax.broadcasted_iota(I32, (2 * blk, 2 * blk), 1)
    qi = jnp.where(row < blk, row, row - blk)
    dist = qi + blk - col
    first_key = jnp.where(first, blk, 0)
    valid = (dist >= 0) & (dist < WINDOW) & (col >= first_key)
    lo_kv = _lane_half((2 * blk, LANES))
    lo_q = _lane_half((blk, LANES))
    rowq = lax.broadcasted_iota(I32, (2 * blk, LANES), 0)
    q_keep = (rowq < blk) == lo_kv
    top_rows = lax.broadcasted_iota(I32, (2 * blk, 1), 0) < blk

    k0 = WB
    v0 = WB + WKV
    kk = jnp.concatenate([prev_ref[:, k0:k0 + WKV], cur_ref[:, k0:k0 + WKV]], axis=0).astype(F32)
    vv = jnp.concatenate([prev_ref[:, v0:v0 + WKV], cur_ref[:, v0:v0 + WKV]], axis=0).astype(F32)
    kk_sw = pltpu.roll(kk, LANES // 2, axis=1)
    vv_sw = pltpu.roll(vv, LANES // 2, axis=1)
    k_dup = (jnp.where(lo_kv, kk, kk_sw), jnp.where(lo_kv, kk_sw, kk))
    v_dup = (jnp.where(lo_kv, vv, vv_sw), jnp.where(lo_kv, vv_sw, vv))

    for p in range(H_B // 2):
        j = (2 * p) // (H_B // H_B_KV)
        qp = cur_ref[:, p * LANES:(p + 1) * LANES].astype(F32) * (DH_B ** -0.5)
        q_st = jnp.where(q_keep, jnp.concatenate([qp, qp], axis=0), 0.0)
        s = _mm_nt(q_st, k_dup[j])
        s = jnp.where(valid, s + bias_ref[p], -1e30)
        sink = jnp.where(top_rows, sink_ref[2 * p], sink_ref[2 * p + 1])
        m = jnp.maximum(jnp.max(s, axis=-1, keepdims=True), sink)
        pe = jnp.exp(s - m)
        den = jnp.sum(pe, axis=-1, keepdims=True) + jnp.exp(sink - m)
        o2 = _mm(pe, v_dup[j]) / den
        o_ref[:, p * LANES:(p + 1) * LANES] = jnp.where(lo_q, o2[0:blk], o2[blk:2 * blk]).astype(BF16)


def _swa(b1, sinks, bias_tab, seq):
    t, nc = b1.shape
    nbs = seq // ATT_BLK
    kern = functools.partial(_swa_kernel, blocks_per_seq=nbs)
    return pl.pallas_call(
        kern,
        grid=(t // ATT_BLK,),
        in_specs=[
            pl.BlockSpec(memory_space=pltpu.SMEM),
            pl.BlockSpec((ATT_BLK, nc), lambda i: (i, 0)),
            pl.BlockSpec((ATT_BLK, nc), lambda i: (jnp.maximum(i - 1, 0), 0)),
            pl.BlockSpec((H_B // 2, 2 * ATT_BLK, 2 * ATT_BLK), lambda i: (0, 0, 0)),
        ],
        out_specs=pl.BlockSpec((ATT_BLK, WB), lambda i: (i, 0)),
        out_shape=jax.ShapeDtypeStruct((t, WB), BF16),
        compiler_params=_cparams(("arbitrary",)),
        name="swa",
    )(sinks, b1, b1, bias_tab)


def _t5_bias_table(rel_bias):
    qi = jnp.arange(ATT_BLK)[:, None]
    ki = jnp.arange(2 * ATT_BLK)[None, :]
    n = jnp.maximum(qi + ATT_BLK - ki, 0)
    max_exact = NUM_BUCKETS // 2
    nf = jnp.maximum(n, 1).astype(F32)
    large = max_exact + (jnp.log(nf / max_exact) / math.log(MAX_DISTANCE / max_exact)
                         * (NUM_BUCKETS - max_exact)).astype(I32)
    bucket = jnp.where(n < max_exact, n, jnp.minimum(large, NUM_BUCKETS - 1))
    onehot = (bucket[..., None] == jnp.arange(NUM_BUCKETS)).astype(F32)
    bias = jnp.einsum('qkb,bh->hqk', onehot, rel_bias.astype(F32), precision=lax.Precision.HIGHEST)
    return bias.reshape(H_B // 2, 2 * ATT_BLK, 2 * ATT_BLK)


def _route_tail(x1, mod_ref, nw2_ref, rwhl_ref, rwhi_ref, rb_ref, tril_ref, carry_ref,
                h2_ref, meta_ref, metat_ref, cnt_ref):
    tm = x1.shape[0]
    h2 = _norm_mod(x1, nw2_ref[...], mod_ref[0, 4:5, :], mod_ref[0, 3:4, :])
    _store_row_tiles(h2_ref, h2)
    h_hi = h2.astype(BF16)
    h_lo = (h2 - h_hi.astype(F32)).astype(BF16)
    r = jnp.dot(h_hi, rwhl_ref[...], preferred_element_type=F32)
    logits = (r[:, 0:LANES] + r[:, LANES:2 * LANES]
              + jnp.dot(h_lo, rwhi_ref[...], preferred_element_type=F32) + rb_ref[...])
    lane = lax.broadcasted_iota(I32, (tm, LANES), 1)
    lane_f = lane.astype(F32)
    big = 1e9
    neg = -jnp.inf
    is_grp = (lane >= R_GRP0) & (lane < R_GRP0 + N_GROUPS)
    lg1 = jnp.where(is_grp, logits, neg)
    m1 = jnp.max(lg1, axis=-1, keepdims=True)
    grp = jnp.min(jnp.where(lg1 == m1, lane_f, big), axis=-1, keepdims=True) - R_GRP0
    p_top = 1.0 / jnp.sum(jnp.where(is_grp, jnp.exp(lg1 - m1), 0.0), axis=-1, keepdims=True)
    lane_grp = jnp.right_shift(lane, EXPERTS_PER_GROUP.bit_length() - 1).astype(F32)
    in_grp = (lane < N_EXPERTS) & (lane_grp == grp)
    l2 = jnp.where(in_grp, logits, neg)
    t1 = jnp.max(l2, axis=-1, keepdims=True)
    e1 = jnp.min(jnp.where(l2 == t1, lane_f, big), axis=-1, keepdims=True)
    l2b = jnp.where(lane_f == e1, neg, l2)
    t2 = jnp.max(l2b, axis=-1, keepdims=True)
    e2 = jnp.min(jnp.where(l2b == t2, lane_f, big), axis=-1, keepdims=True)
    ex = jnp.exp(t2 - t1)
    gate1 = p_top / (1.0 + ex)
    gate2 = p_top * ex / (1.0 + ex)
    hit1 = lane_f == e1
    hit2 = lane_f == e2
    onehot = jnp.where(hit1 | hit2, 1.0, 0.0)
    incl = jnp.dot(tril_ref[...], onehot.astype(BF16), preferred_element_type=F32)
    excl = incl - onehot + carry_ref[...]
    rank1 = jnp.sum(jnp.where(hit1, excl, 0.0), axis=-1, keepdims=True)
    rank2 = jnp.sum(jnp.where(hit2, excl, 0.0), axis=-1, keepdims=True)
    carry_ref[...] = carry_ref[...] + incl[tm - 1:tm, :]
    cnt_ref[...] = carry_ref[...]
    meta = jnp.zeros((tm, LANES), F32)
    for idx, val in ((M_E1, e1), (M_E2, e2), (M_R1, rank1), (M_R2, rank2), (M_G1, gate1), (M_G2, gate2)):
        meta = jnp.where(lane == idx, val, meta)
    meta_ref[...] = meta
    metat_ref[0] = meta.T[0:8, :]


def _chunk_rows(cidx, n):
    return pl.ds(cidx, n, stride=ROW_TILE)


def _store_row_tiles(ref, x):
    for cidx in range(ROW_TILE):
        ref[_chunk_rows(cidx, x.shape[0]), :] = x[:, cidx * LANES:(cidx + 1) * LANES]


def _load_row_tiles(ref):
    n = ref.shape[0] // ROW_TILE
    return jnp.concatenate([ref[_chunk_rows(cidx, n), :] for cidx in range(ROW_TILE)], axis=1)


def _tail_in_specs(d, tm, tps):
    return [
        pl.BlockSpec((1, 6, d), lambda i: (i // tps, 0, 0)),
        pl.BlockSpec((1, d), lambda i: (0, 0)),
        pl.BlockSpec((d, 2 * LANES), lambda i: (0, 0)),
        pl.BlockSpec((d, LANES), lambda i: (0, 0)),
        pl.BlockSpec((1, LANES), lambda i: (0, 0)),
        pl.BlockSpec((tm, tm), lambda i: (0, 0)),
    ]


def _tail_out(t, d, tm):
    specs = [
        pl.BlockSpec((tm, d), lambda i: (i, 0)),
        pl.BlockSpec((tm * ROW_TILE, LANES), lambda i: (i, 0)),
        pl.BlockSpec((tm, LANES), lambda i: (i, 0)),
        pl.BlockSpec((1, 8, tm), lambda i: (i, 0, 0)),
        pl.BlockSpec((1, LANES), lambda i: (0, 0)),
    ]
    shapes = [
        jax.ShapeDtypeStruct((t, d), F32),
        jax.ShapeDtypeStruct((t * ROW_TILE, LANES), F32),
        jax.ShapeDtypeStruct((t, LANES), F32),
        jax.ShapeDtypeStruct((t // tm, 8, tm), F32),
        jax.ShapeDtypeStruct((1, LANES), F32),
    ]
    return specs, shapes


def _outproj_kernel(oa_ref, ob_ref, x_ref, wo_ref, mod_ref, nw2_ref, rwhl_ref, rwhi_ref, rb_ref,
                    tril_ref, x1_ref, h2_ref, meta_ref, metat_ref, cnt_ref, carry_ref):
    @pl.when(pl.program_id(0) == 0)
    def _():
        carry_ref[...] = jnp.zeros(carry_ref.shape, F32)

    y = (jnp.dot(oa_ref[...], wo_ref[0:WA, :], preferred_element_type=F32)
         + jnp.dot(ob_ref[...], wo_ref[WA:WA + WB, :], preferred_element_type=F32))
    x1 = x_ref[...] + mod_ref[0, 2:3, :] * y
    x1_ref[...] = x1
    _route_tail(x1, mod_ref, nw2_ref, rwhl_ref, rwhi_ref, rb_ref, tril_ref, carry_ref,
                h2_ref, meta_ref, metat_ref, cnt_ref)


def _outproj_route(oa, ob, x, w_out_b, mod, nw2, rwhl, rwhi, rb, tril, seq, tm):
    t, d = x.shape
    tps = seq // tm
    out_specs, out_shapes = _tail_out(t, d, tm)
    return pl.pallas_call(
        _outproj_kernel,
        grid=(t // tm,),
        in_specs=[
            pl.BlockSpec((tm, WA), lambda i: (i, 0)),
            pl.BlockSpec((tm, WB), lambda i: (i, 0)),
            pl.BlockSpec((tm, d), lambda i: (i, 0)),
            pl.BlockSpec((WA + WB, d), lambda i: (0, 0)),
        ] + _tail_in_specs(d, tm, tps),
        out_specs=out_specs,
        out_shape=out_shapes,
        scratch_shapes=[pltpu.VMEM((1, LANES), F32)],
        compiler_params=_cparams(("arbitrary",)),
        name="outproj_route",
    )(oa, ob, x, w_out_b, mod, nw2, rwhl, rwhi, rb, tril)


def _pool_kernel(x_ref, nw1_ref, pw_ref, ps_ref, mod_ref, nw2_ref, rwhl_ref, rwhi_ref, rb_ref,
                 tril_ref, x1_ref, h2_ref, meta_ref, metat_ref, cnt_ref, carry_ref, e1_ref, e2_ref, e3_ref,
                 *, blocks_per_seq):
    tp, d = x_ref.shape
    hist = POOL_HIST
    i = pl.program_id(0)

    @pl.when(i == 0)
    def _():
        carry_ref[...] = jnp.zeros(carry_ref.shape, F32)

    @pl.when(i % blocks_per_seq == 0)
    def _():
        e1_ref[0:hist, :] = jnp.zeros((hist, d), F32)

    x = x_ref[...]
    h = _norm_mod(x, nw1_ref[...], mod_ref[0, 1:2, :], mod_ref[0, 0:1, :])
    e1_ref[hist:hist + tp, :] = h
    pos = (i % blocks_per_seq) * tp + lax.broadcasted_iota(I32, (tp, 1), 0)
    gdim = POOL_GROUP_DIM
    ys = []
    for gi, win in enumerate(POOL_WINDOWS):
        c0 = gi * gdim
        levels = win.bit_length() - 1
        src, spare = e1_ref, (e2_ref, e3_ref)
        for m in range(1, levels + 1):
            shift = 1 << (m - 1)
            start = hist - 8 * (levels - m)
            n = hist + tp - start
            cur = src[start:start + n, c0:c0 + gdim] + src[start - shift:start - shift + n, c0:c0 + gdim]
            if m < levels:
                dst = spare[m % 2]
                dst[start:start + n, c0:c0 + gdim] = cur
                src = dst
        cnt = jnp.minimum(pos + 1, win).astype(F32)
        pooled = cur / cnt - h[:, c0:c0 + gdim]
        ys.append(_mm(pooled, pw_ref[gi]))
    e1_ref[0:hist, :] = h[tp - hist:tp, :]
    y = jnp.concatenate(ys, axis=1) * ps_ref[...]
    x1 = x + mod_ref[0, 2:3, :] * y
    x1_ref[...] = x1
    _route_tail(x1, mod_ref, nw2_ref, rwhl_ref, rwhi_ref, rb_ref, tril_ref, carry_ref,
                h2_ref, meta_ref, metat_ref, cnt_ref)


def _pool_route(x, nw1, pool_w_b, pool_scale, mod, nw2, rwhl, rwhi, rb, tril, seq, tp):
    t, d = x.shape
    tps = seq // tp
    out_specs, out_shapes = _tail_out(t, d, tp)
    kern = functools.partial(_pool_kernel, blocks_per_seq=tps)
    return pl.pallas_call(
        kern,
        grid=(t // tp,),
        in_specs=[
            pl.BlockSpec((tp, d), lambda i: (i, 0)),
            pl.BlockSpec((1, d), lambda i: (0, 0)),
            pl.BlockSpec((4, POOL_GROUP_DIM, POOL_GROUP_DIM), lambda i: (0, 0, 0)),
            pl.BlockSpec((1, d), lambda i: (0, 0)),
        ] + _tail_in_specs(d, tp, tps),
        out_specs=out_specs,
        out_shape=out_shapes,
        scratch_shapes=[
            pltpu.VMEM((1, LANES), F32),
            pltpu.VMEM((tp + POOL_HIST, d), F32),
            pltpu.VMEM((tp + POOL_HIST, d), F32),
            pltpu.VMEM((tp + POOL_HIST, d), F32),
        ],
        compiler_params=_cparams(("arbitrary",)),
        name="pool_route",
    )(x, nw1, pool_w_b, pool_scale, mod, nw2, rwhl, rwhi, rb, tril)


def _dispatch_kernel(dest_ref, fill_ref, h_ref, xs_ref, zero_ref, sem):
    td = h_ref.shape[0] // ROW_TILE

    def row_tile(r, n=1):
        return pl.ds(pl.multiple_of(r * ROW_TILE, ROW_TILE), n * ROW_TILE)

    @pl.when(pl.program_id(0) == 0)
    def _():
        zero_ref[...] = jnp.zeros(zero_ref.shape, zero_ref.dtype)

        def fill(e, carry):
            pos = fill_ref[0, e]
            n = fill_ref[1, e]
            for bit in reversed(range(MOE_ROWS.bit_length() - 1)):
                size = 1 << bit
                take = (n & size) != 0

                @pl.when(take)
                def _():
                    pltpu.make_async_copy(zero_ref.at[row_tile(0, size)], xs_ref.at[row_tile(pos, size)], sem).start()

                pos = pos + jnp.where(take, size, 0)
            return carry

        lax.fori_loop(0, N_EXPERTS, fill, 0)

        def drain_fill(r, carry):
            pltpu.make_async_copy(zero_ref.at[row_tile(0)], xs_ref.at[row_tile(0)], sem).wait()
            return carry

        lax.fori_loop(0, fill_ref[1, N_EXPERTS], drain_fill, 0)

    def issue(tk, carry):
        for s in range(2):
            d = dest_ref[0, 0, s * td + tk]
            pltpu.make_async_copy(h_ref.at[row_tile(tk)], xs_ref.at[row_tile(d)], sem).start(priority=s)
        return carry

    lax.fori_loop(0, td, issue, 0)

    def drain(tk, carry):
        pltpu.make_async_copy(h_ref.at[row_tile(0)], xs_ref.at[row_tile(0)], sem).wait()
        return carry

    lax.fori_loop(0, 2 * td, drain, 0)


def _dispatch(dest3, fill, h2, n_rows, td):
    t = h2.shape[0] // ROW_TILE
    return pl.pallas_call(
        _dispatch_kernel,
        grid=(t // td,),
        in_specs=[
            pl.BlockSpec((1, 1, 2 * td), lambda i: (i, 0, 0), memory_space=pltpu.SMEM),
            pl.BlockSpec(memory_space=pltpu.SMEM),
            pl.BlockSpec((td * ROW_TILE, LANES), lambda i: (i, 0)),
        ],
        out_specs=pl.BlockSpec(memory_space=pl.ANY),
        out_shape=jax.ShapeDtypeStruct((n_rows * ROW_TILE, LANES), h2.dtype),
        scratch_shapes=[pltpu.VMEM((MOE_ROWS // 2 * ROW_TILE, LANES), h2.dtype), pltpu.SemaphoreType.DMA(())],
        compiler_params=_cparams(("arbitrary",)),
        name="moe_dispatch",
    )(dest3, fill, h2)


def _expert_kernel(be_ref, nu_ref, xs_ref, wg_ref, wu_ref, wd_ref, y_ref, wgb_ref, wub_ref, wdb_ref):
    i = pl.program_id(0)
    prev = be_ref[jnp.maximum(i - 1, 0)]
    changed = (i == 0) | (be_ref[i] != prev)

    @pl.when(changed)
    def _():
        wgb_ref[...] = wg_ref[0, 0].astype(BF16)
        wub_ref[...] = wu_ref[0, 0].astype(BF16)
        wdb_ref[...] = wd_ref[0, 0].astype(BF16)

    @pl.when(i < nu_ref[0])
    def _():
        xb = _load_row_tiles(xs_ref).astype(BF16)
        g = jnp.dot(xb, wgb_ref[...], preferred_element_type=F32)
        u = jnp.dot(xb, wub_ref[...], preferred_element_type=F32)
        a = (_silu(g) * u).astype(BF16)
        _store_row_tiles(y_ref, jnp.dot(a, wdb_ref[...], preferred_element_type=F32))

    @pl.when(i >= nu_ref[0])
    def _():
        y_ref[...] = jnp.zeros(y_ref.shape, F32)


def _experts(blk_expert, n_used, xs, w_gate, w_up, w_down, layer):
    n_rows = xs.shape[0] // ROW_TILE
    d = ROW_TILE * LANES
    blk = (MOE_ROWS * ROW_TILE, LANES)
    n_blk = n_rows // MOE_ROWS
    return pl.pallas_call(
        _expert_kernel,
        grid_spec=pltpu.PrefetchScalarGridSpec(
            num_scalar_prefetch=2,
            grid=(n_blk,),
            in_specs=[
                pl.BlockSpec(blk, lambda i, be, nu: (i, 0)),
                pl.BlockSpec((1, 1, d, D_FF), lambda i, be, nu: (layer, be[i], 0, 0)),
                pl.BlockSpec((1, 1, d, D_FF), lambda i, be, nu: (layer, be[i], 0, 0)),
                pl.BlockSpec((1, 1, D_FF, d), lambda i, be, nu: (layer, be[i], 0, 0)),
            ],
            out_specs=pl.BlockSpec(blk, lambda i, be, nu: (i, 0)),
            scratch_shapes=[
                pltpu.VMEM((d, D_FF), BF16),
                pltpu.VMEM((d, D_FF), BF16),
                pltpu.VMEM((D_FF, d), BF16),
            ],
        ),
        out_shape=jax.ShapeDtypeStruct(xs.shape, F32),
        compiler_params=_cparams(("arbitrary",)),
        name="moe_experts",
    )(blk_expert, n_used, xs, w_gate, w_up, w_down)


def _combine_kernel(dest_ref, ys_ref, x1_ref, meta_ref, mod_ref, fw_ref, o_ref, buf_ref, sem, *, final_norm):
    tc = x1_ref.shape[0]

    def row_tile(r):
        return pl.ds(pl.multiple_of(r * ROW_TILE, ROW_TILE), ROW_TILE)

    def issue(tk, carry):
        for s in range(2):
            d = dest_ref[0, 0, s * tc + tk]
            pltpu.make_async_copy(ys_ref.at[row_tile(d)], buf_ref.at[s, row_tile(tk)], sem).start(priority=s)
        return carry

    lax.fori_loop(0, tc, issue, 0)

    def drain(tk, carry):
        pltpu.make_async_copy(ys_ref.at[row_tile(0)], buf_ref.at[0, row_tile(0)], sem).wait()
        return carry

    lax.fori_loop(0, 2 * tc, drain, 0)

    meta = meta_ref[...]
    g1 = meta[:, M_G1:M_G1 + 1]
    g2 = meta[:, M_G2:M_G2 + 1]
    n_chunks = ROW_TILE
    ssq = jnp.zeros((tc, 1), F32)
    for cidx in range(n_chunks):
        sl = slice(cidx * LANES, (cidx + 1) * LANES)
        rows = _chunk_rows(cidx, tc)
        moe = buf_ref[0, rows, :] * g1 + buf_ref[1, rows, :] * g2
        x2 = x1_ref[:, sl] + mod_ref[0, 5:6, sl] * moe
        o_ref[:, sl] = x2
        if final_norm:
            ssq = ssq + jnp.sum(x2 * x2, axis=-1, keepdims=True)
    if final_norm:
        inv = lax.rsqrt(ssq * (1.0 / (n_chunks * LANES)) + EPS)
        for cidx in range(n_chunks):
            sl = slice(cidx * LANES, (cidx + 1) * LANES)
            o_ref[:, sl] = o_ref[:, sl] * inv * fw_ref[:, sl]


def _combine(dest3, ys, x1, meta, mod, fw, seq, tc, final_norm):
    t, d = x1.shape
    tps = seq // tc
    kern = functools.partial(_combine_kernel, final_norm=final_norm)
    return pl.pallas_call(
        kern,
        grid=(t // tc,),
        in_specs=[
            pl.BlockSpec((1, 1, 2 * tc), lambda i: (i, 0, 0), memory_space=pltpu.SMEM),
            pl.BlockSpec(memory_space=pl.ANY),
            pl.BlockSpec((tc, d), lambda i: (i, 0)),
            pl.BlockSpec((tc, LANES), lambda i: (i, 0)),
            pl.BlockSpec((1, 6, d), lambda i: (i // tps, 0, 0)),
            pl.BlockSpec((1, d), lambda i: (0, 0)),
        ],
        out_specs=pl.BlockSpec((tc, d), lambda i: (i, 0)),
        out_shape=jax.ShapeDtypeStruct((t, d), F32),
        scratch_shapes=[pltpu.VMEM((2, tc * ROW_TILE, LANES), F32), pltpu.SemaphoreType.DMA(())],
        compiler_params=_cparams(("arbitrary",)),
        name="moe_combine",
    )(dest3, ys, x1, meta, mod, fw)


def _moe(x1, h2, meta, metat, cnt, mod, fw, w_gate, w_up, w_down, layer, seq, tile, final_norm):
    t, d = x1.shape
    n_rows = -(-(2 * t + N_EXPERTS * (MOE_ROWS - 1)) // MOE_ROWS) * MOE_ROWS
    n_blk = n_rows // MOE_ROWS
    counts = cnt[0, :N_EXPERTS].astype(I32)
    padded = (counts + MOE_ROWS - 1) // MOE_ROWS * MOE_ROWS
    pad_end = jnp.cumsum(padded)
    pad_start = pad_end - padded
    expert = metat[:, M_E1:M_E2 + 1, :].astype(I32)
    rank = metat[:, M_R1:M_R2 + 1, :].astype(I32)
    start_of = jnp.sum(jnp.where(expert[..., None] == jnp.arange(N_EXPERTS, dtype=I32), pad_start, 0), axis=-1)
    dest3 = (start_of + rank).reshape(t // tile, 1, 2 * tile)
    blk_row0 = jnp.arange(n_blk, dtype=I32) * MOE_ROWS
    blk_expert = jnp.minimum(jnp.sum((pad_end[None, :] <= blk_row0[:, None]).astype(I32), axis=1),
                             N_EXPERTS - 1)
    n_used = (pad_end[-1:] // MOE_ROWS).astype(I32)
    n_fill = padded - counts
    fill = jnp.stack([jnp.concatenate([pad_start + counts, jnp.zeros((1,), I32)]),
                      jnp.concatenate([n_fill, jnp.sum(n_fill, keepdims=True)])])
    xs = _dispatch(dest3, fill, h2, n_rows, tile)
    ys = _experts(blk_expert, n_used, xs, w_gate, w_up, w_down, layer)
    return _combine(dest3, ys, x1, meta, mod, fw, seq, tile, final_norm)


def _router_params(r1_w, r1_b, r2_w, r2_b):
    d = r1_w.shape[0]
    rw = jnp.zeros((d, LANES), F32).at[:, :N_EXPERTS].set(r2_w).at[:, R_GRP0:R_GRP0 + N_GROUPS].set(r1_w)
    rb = jnp.zeros((1, LANES), F32).at[0, :N_EXPERTS].set(r2_b).at[0, R_GRP0:R_GRP0 + N_GROUPS].set(r1_b)
    hi = rw.astype(BF16)
    lo = (rw - hi.astype(F32)).astype(BF16)
    return jnp.concatenate([hi, lo], axis=1), hi, rb


def kernel(x, c, ada_w, ada_b, norm_mix_w, norm_ffn_w, ab_w_in, ab_conv_w, ab_a_log, ab_dt_bias, ab_onorm_w, ab_sinks, rel_bias, ab_w_out, pool_w, pool_scale, r1_w, r1_b, r2_w, r2_b, moe_w_gate, moe_w_up, moe_w_down, final_norm_w):
    bsz, seq, d = x.shape
    t = bsz * seq
    tile = min(512, seq)
    lb = min(256, seq)
    xf = x.reshape(t, d)

    mod_all = _ada(c, ada_w, ada_b).reshape(DEPTH, bsz, 6, d)
    tril = jnp.tril(jnp.ones((tile, tile), BF16))
    fw = final_norm_w.reshape(1, d)

    w_in = ab_w_in[0]
    cuts = (0, WA, 2 * WA, 3 * WA, 4 * WA, 4 * WA + H_A, 4 * WA + 2 * H_A)
    qkvz = w_in[:, :cuts[4]]
    ba = w_in[:, cuts[4]:cuts[6]]
    qkv_b = w_in[:, cuts[6]:]
    w_all = jnp.concatenate([qkvz, qkv_b, ba, jnp.zeros((d, LANES - 2 * H_A), F32)], axis=1).astype(BF16)
    a1, b1, g1 = _inproj(xf, mod_all[0], norm_mix_w[0:1], w_all, seq, tile)

    pad = jnp.zeros((LANES - 2 * H_A,), F32)
    alog_row = jnp.concatenate([jnp.zeros((H_A,), F32), ab_a_log[0], pad]).reshape(1, LANES)
    dt_row = jnp.concatenate([jnp.zeros((H_A,), F32), ab_dt_bias[0], pad]).reshape(1, LANES)
    onw_row = jnp.concatenate([ab_onorm_w[0], ab_onorm_w[0]]).reshape(1, LANES)
    oa = _deltanet(a1, g1, ab_conv_w[0], alog_row, dt_row, onw_row, seq, lb)
    ob = _swa(b1, ab_sinks[0], _t5_bias_table(rel_bias), seq)

    rwhl, rwhi, rb = _router_params(r1_w[0], r1_b[0], r2_w[0], r2_b[0])
    x1, h2, meta, metat, cnt = _outproj_route(oa, ob, xf, ab_w_out[0].astype(BF16), mod_all[0], norm_ffn_w[0:1],
                                       rwhl, rwhi, rb, tril, seq, tile)
    x2 = _moe(x1, h2, meta, metat, cnt, mod_all[0], fw, moe_w_gate, moe_w_up, moe_w_down, 0,
              seq, tile, False)

    rwhl, rwhi, rb = _router_params(r1_w[1], r1_b[1], r2_w[1], r2_b[1])
    x3, h2, meta, metat, cnt = _pool_route(x2, norm_mix_w[1:2], pool_w[0].astype(BF16), pool_scale[0].reshape(1, d),
                                    mod_all[1], norm_ffn_w[1:2], rwhl, rwhi, rb, tril, seq, tile)
    out = _moe(x3, h2, meta, metat, cnt, mod_all[1], fw, moe_w_gate, moe_w_up, moe_w_down, 1,
               seq, tile, True)
    return out.reshape(bsz, seq, d)
```

```python
import functools
import math

import jax
import jax.numpy as jnp
from jax import lax
from jax.experimental import pallas as pl
from jax.experimental.pallas import tpu as pltpu

F32 = jnp.float32
BF16 = jnp.bfloat16
I32 = jnp.int32

D_MODEL = 1024
DEPTH = 2
H_A = 8
DK_A = 64
CONV_K = 4
CHUNK = 64
H_B = 8
H_B_KV = 2
DH_B = 64
WINDOW = 128
ATT_BLK = 128
NUM_BUCKETS = 32
MAX_DISTANCE = 128
POOL_WINDOWS = (2, 4, 8, 16)
POOL_GROUP_DIM = D_MODEL // 4
POOL_HIST = 32
N_GROUPS = 4
EXPERTS_PER_GROUP = 8
N_EXPERTS = N_GROUPS * EXPERTS_PER_GROUP
D_FF = D_MODEL // 2
EPS = 1e-6

WA = H_A * DK_A
WB = H_B * DH_B
WKV = H_B_KV * DH_B
N_PAIR = H_A // 2
LANES = 128
ROW_TILE = D_MODEL // LANES
MOE_ROWS = 512
VMEM_LIMIT = 56 * 1024 * 1024

M_E1, M_E2, M_R1, M_R2, M_G1, M_G2 = 0, 1, 2, 3, 4, 5
R_GRP0 = N_EXPERTS


def _cparams(sem):
    return pltpu.CompilerParams(dimension_semantics=sem, vmem_limit_bytes=VMEM_LIMIT)


def _mm(a, b):
    return jnp.dot(a.astype(BF16), b.astype(BF16), preferred_element_type=F32)


def _mm_nt(a, b):
    return lax.dot_general(a.astype(BF16), b.astype(BF16), (((1,), (1,)), ((), ())),
                           preferred_element_type=F32)


def _mm_x3(a, b):
    ah = a.astype(BF16)
    al = (a - ah.astype(F32)).astype(BF16)
    bh = b.astype(BF16)
    bl = (b - bh.astype(F32)).astype(BF16)
    return jnp.dot(jnp.concatenate([ah, al, ah], axis=1), jnp.concatenate([bh, bh, bl], axis=0),
                   preferred_element_type=F32)


def _split3(x):
    hi = x.astype(BF16)
    r = x - hi.astype(F32)
    mid = r.astype(BF16)
    lo = (r - mid.astype(F32)).astype(BF16)
    return hi, mid, lo


def _norm_mod(x, nw, sc, sh):
    ms = jnp.mean(x * x, axis=-1, keepdims=True)
    return (x * lax.rsqrt(ms + EPS) * nw) * (1.0 + sc) + sh


def _silu(x):
    return x * jax.nn.sigmoid(x)


def _ada_kernel(ct_ref, w_ref, b_ref, o_ref):
    nb = ct_ref.shape[1]
    d = ct_ref.shape[0]
    rows = []
    for b in range(nb):
        acc = None
        for k0 in range(0, d, LANES):
            ct = ct_ref[k0:k0 + LANES, b:b + 1]
            part = jnp.sum(_silu(ct) * w_ref[0, k0:k0 + LANES, :], axis=0, keepdims=True)
            acc = part if acc is None else acc + part
        rows.append(acc)
    o_ref[0] = jnp.concatenate(rows, axis=0) + b_ref[0]


def _ada(c, ada_w, ada_b):
    nb, d = c.shape
    depth, _, n6 = ada_w.shape
    cols = 1536
    return pl.pallas_call(
        _ada_kernel,
        grid=(depth, n6 // cols),
        in_specs=[
            pl.BlockSpec((d, nb), lambda l, j: (0, 0)),
            pl.BlockSpec((1, d, cols), lambda l, j: (l, 0, j)),
            pl.BlockSpec((1, 1, cols), lambda l, j: (l, 0, j)),
        ],
        out_specs=pl.BlockSpec((1, nb, cols), lambda l, j: (l, 0, j)),
        out_shape=jax.ShapeDtypeStruct((depth, nb, n6), F32),
        compiler_params=_cparams(("arbitrary", "arbitrary")),
        name="ada_mod",
    )(c.T, ada_w, ada_b.reshape(depth, 1, n6))


def _inproj_kernel(x_ref, mod_ref, nw_ref, w_ref, a_ref, b_ref, g_ref):
    h = _norm_mod(x_ref[...], nw_ref[...], mod_ref[0, 1:2, :], mod_ref[0, 0:1, :])
    hb = h.astype(BF16)
    na = a_ref.shape[1]
    nbb = b_ref.shape[1]
    a_ref[...] = jnp.dot(hb, w_ref[:, 0:na], preferred_element_type=F32)
    b_ref[...] = jnp.dot(hb, w_ref[:, na:na + nbb], preferred_element_type=F32).astype(BF16)
    g_ref[...] = jnp.dot(hb, w_ref[:, na + nbb:], preferred_element_type=F32)


def _inproj(x, mod, nw, w_all, seq, tm):
    t, d = x.shape
    tps = seq // tm
    na, nbb, ng = 4 * WA, WB + 2 * WKV, LANES
    return pl.pallas_call(
        _inproj_kernel,
        grid=(t // tm,),
        in_specs=[
            pl.BlockSpec((tm, d), lambda i: (i, 0)),
            pl.BlockSpec((1, 6, d), lambda i: (i // tps, 0, 0)),
            pl.BlockSpec((1, d), lambda i: (0, 0)),
            pl.BlockSpec((d, na + nbb + ng), lambda i: (0, 0)),
        ],
        out_specs=[
            pl.BlockSpec((tm, na), lambda i: (i, 0)),
            pl.BlockSpec((tm, nbb), lambda i: (i, 0)),
            pl.BlockSpec((tm, ng), lambda i: (i, 0)),
        ],
        out_shape=[
            jax.ShapeDtypeStruct((t, na), F32),
            jax.ShapeDtypeStruct((t, nbb), BF16),
            jax.ShapeDtypeStruct((t, ng), F32),
        ],
        compiler_params=_cparams(("arbitrary",)),
        name="inproj",
    )(x, mod, nw, w_all)


def _lane_half(shape):
    return lax.broadcasted_iota(I32, shape, 1) < (LANES // 2)


def _half_sums(x2, lo):
    se = jnp.sum(jnp.where(lo, x2, 0.0), axis=-1, keepdims=True)
    so = jnp.sum(jnp.where(lo, 0.0, x2), axis=-1, keepdims=True)
    return jnp.where(lo, se, so)


def _deltanet_kernel(a_ref, gp_ref, cw_ref, alog_ref, dt_ref, onw_ref, o_ref,
                     ext_ref, s_ref, q_ref, k_ref, v_ref, gf_ref, bf_ref, *, blocks_per_seq):
    lb = a_ref.shape[0]
    i = pl.program_id(0)

    @pl.when(i % blocks_per_seq == 0)
    def _():
        ext_ref[0:8, :] = jnp.zeros((8, ext_ref.shape[1]), F32)
        s_ref[...] = jnp.zeros(s_ref.shape, F32)

    lo_b = _lane_half((lb, LANES))
    for part, dst in ((0, q_ref), (1, k_ref), (2, v_ref)):
        for p in range(N_PAIR):
            c0 = part * WA + p * LANES
            xin = a_ref[:, c0:c0 + LANES]
            ext_ref[8:8 + lb, c0:c0 + LANES] = xin
            acc = cw_ref[CONV_K - 1:CONV_K, c0:c0 + LANES] * xin
            for j in range(CONV_K - 1):
                acc = acc + cw_ref[j:j + 1, c0:c0 + LANES] * ext_ref[pl.ds(8 - (CONV_K - 1) + j, lb), c0:c0 + LANES]
            ext_ref[0:8, c0:c0 + LANES] = xin[lb - 8:lb, :]
            y = _silu(acc)
            if part < 2:
                y = y * lax.rsqrt(_half_sums(y * y, lo_b) + EPS)
                if part == 0:
                    y = y * (DK_A ** -0.5)
            dst[:, p * LANES:(p + 1) * LANES] = y

    gp = gp_ref[...]
    bf_ref[...] = jax.nn.sigmoid(gp)
    xg = gp + dt_ref[...]
    softplus = jnp.maximum(xg, 0.0) + jnp.log(1.0 + jnp.exp(-jnp.abs(xg)))
    gf_ref[...] = -jnp.exp(alog_ref[...]) * softplus

    c = CHUNK
    c2 = 2 * c
    row = lax.broadcasted_iota(I32, (c2, c2), 0)
    col = lax.broadcasted_iota(I32, (c2, c2), 1)
    same_blk = (row < c) == (col < c)
    incl = same_blk & (row >= col)
    strict = same_blk & (row > col)
    eye = row == col
    eye_f = jnp.where(eye, 1.0, 0.0)
    lo = _lane_half((c, LANES))

    def stack(x):
        return jnp.where(same_blk, jnp.concatenate([x, x], axis=0), 0.0)

    def unstack(x):
        return jnp.where(lo, x[0:c], x[c:c2])

    rb = lax.broadcasted_iota(I32, (lb, lb), 0)
    cb = lax.broadcasted_iota(I32, (lb, lb), 1)
    shift = CHUNK.bit_length() - 1
    tril_chunks = jnp.where((jnp.right_shift(rb, shift) == jnp.right_shift(cb, shift)) & (rb >= cb),
                            1.0, 0.0).astype(BF16)
    g_hi, g_mid, g_lo = _split3(gf_ref[...])
    gf_ref[...] = (jnp.dot(tril_chunks, g_hi, preferred_element_type=F32)
                   + jnp.dot(tril_chunks, g_mid, preferred_element_type=F32)
                   + jnp.dot(tril_chunks, g_lo, preferred_element_type=F32))

    pairs = range(N_PAIR)

    def chunk_body(ci, carry):
        r0 = pl.multiple_of(ci * c, c)
        gcc = gf_ref[pl.ds(r0, c), :]
        bfc = bf_ref[pl.ds(r0, c), :]
        q = [q_ref[pl.ds(r0, c), p * LANES:(p + 1) * LANES] for p in pairs]
        k = [k_ref[pl.ds(r0, c), p * LANES:(p + 1) * LANES] for p in pairs]
        v = [v_ref[pl.ds(r0, c), p * LANES:(p + 1) * LANES] for p in pairs]

        def lane_bcast(x, j):
            return jnp.broadcast_to(x[:, j:j + 1], (c, LANES))

        beta = [jnp.where(lo, lane_bcast(bfc, 2 * p), lane_bcast(bfc, 2 * p + 1)) for p in pairs]
        gce = [lane_bcast(gcc, H_A + 2 * p) for p in pairs]
        gco = [lane_bcast(gcc, H_A + 2 * p + 1) for p in pairs]
        gcol = [jnp.concatenate([gce[p], gco[p]], axis=0) for p in pairs]
        gc = [jnp.where(lo, gce[p], gco[p]) for p in pairs]
        grow = [jnp.sum(jnp.where(eye, gcol[p], 0.0), axis=0, keepdims=True) for p in pairs]
        decay = [jnp.where(incl, jnp.exp(jnp.where(incl, gcol[p] - grow[p], 0.0)), 0.0) for p in pairs]
        egc = [jnp.exp(gc[p]) for p in pairs]
        kb = [k[p] * beta[p] for p in pairs]
        k_st = [stack(k[p]) for p in pairs]
        n_mat = [jnp.where(strict, _mm_nt(stack(kb[p]), k_st[p]) * decay[p], 0.0) for p in pairs]
        a_qk = [_mm_nt(stack(q[p]), k_st[p]) * decay[p] for p in pairs]
        x_inv = [eye_f - n_mat[p] for p in pairs]
        y_pow = [_mm_x3(n_mat[p], n_mat[p]) for p in pairs]
        for it in range(5):
            x_inv = [x_inv[p] + _mm_x3(x_inv[p], y_pow[p]) for p in pairs]
            if it < 4:
                y_pow = [_mm_x3(y_pow[p], y_pow[p]) for p in pairs]
        sol = []
        for p in pairs:
            vb = v[p] * beta[p]
            kbg = kb[p] * egc[p]
            rhs = jnp.concatenate([jnp.concatenate([vb, vb], axis=0),
                                   jnp.concatenate([kbg, kbg], axis=0)], axis=1)
            sol.append(_mm(x_inv[p], rhs))
        u = [unstack(sol[p][:, 0:LANES]) for p in pairs]
        w = [unstack(sol[p][:, LANES:2 * LANES]) for p in pairs]
        state = [s_ref[p] for p in pairs]
        wq = [_mm(jnp.concatenate([w[p], q[p] * egc[p]], axis=0), state[p]) for p in pairs]
        v_new = [u[p] - wq[p][0:c] for p in pairs]
        av = [_mm(a_qk[p], jnp.concatenate([v_new[p], v_new[p]], axis=0)) for p in pairs]
        for p in pairs:
            k_dec = k[p] * jnp.exp(gc[p][c - 1:c, :] - gc[p])
            g_tot = jnp.where(row < c, jnp.exp(gcol[p][c - 1:c, :]), jnp.exp(gcol[p][c2 - 1:c2, :]))
            s_ref[p] = state[p] * g_tot + jnp.where(same_blk, _mm(k_dec.T, v_new[p]), 0.0)
        for p in pairs:
            out = wq[p][c:c2] + unstack(av[p])
            ms = _half_sums(out * out, lo) * (1.0 / DK_A)
            z = a_ref[pl.ds(r0, c), 3 * WA + p * LANES:3 * WA + (p + 1) * LANES]
            res = out * lax.rsqrt(ms + EPS) * onw_ref[...] * _silu(z)
            o_ref[pl.ds(r0, c), p * LANES:(p + 1) * LANES] = res.astype(BF16)
        return carry

    lax.fori_loop(0, lb // c, chunk_body, 0, unroll=True)


def _deltanet(a1, g1, conv_w, alog_row, dt_row, onw_row, seq, lb):
    t = a1.shape[0]
    kern = functools.partial(_deltanet_kernel, blocks_per_seq=seq // lb)
    return pl.pallas_call(
        kern,
        grid=(t // lb,),
        in_specs=[
            pl.BlockSpec((lb, 4 * WA), lambda i: (i, 0)),
            pl.BlockSpec((lb, LANES), lambda i: (i, 0)),
            pl.BlockSpec((CONV_K, 3 * WA), lambda i: (0, 0)),
            pl.BlockSpec((1, LANES), lambda i: (0, 0)),
            pl.BlockSpec((1, LANES), lambda i: (0, 0)),
            pl.BlockSpec((1, LANES), lambda i: (0, 0)),
        ],
        out_specs=pl.BlockSpec((lb, WA), lambda i: (i, 0)),
        out_shape=jax.ShapeDtypeStruct((t, WA), BF16),
        scratch_shapes=[
            pltpu.VMEM((lb + 8, 3 * WA), F32),
            pltpu.VMEM((N_PAIR, LANES, LANES), F32),
            pltpu.VMEM((lb, WA), F32),
            pltpu.VMEM((lb, WA), F32),
            pltpu.VMEM((lb, WA), F32),
            pltpu.VMEM((lb, LANES), F32),
            pltpu.VMEM((lb, LANES), F32),
        ],
        compiler_params=_cparams(("arbitrary",)),
        name="deltanet",
    )(a1, g1, conv_w, alog_row, dt_row, onw_row)


def _swa_kernel(sink_ref, cur_ref, prev_ref, bias_ref, o_ref, *, blocks_per_seq):
    blk = ATT_BLK
    i = pl.program_id(0)
    first = (i % blocks_per_seq) == 0
    row = lax.broadcasted_iota(I32, (2 * blk, 2 * blk), 0)
    col = lax.broadcasted_iota(I32, (2 * blk, 2 * blk), 1)
    qi = jnp.where(row < blk, row, row - blk)
    dist = qi + blk - col
    first_key = jnp.where(first, blk, 0)
    valid = (dist >= 0) & (dist < WINDOW) & (col >= first_key)
    lo_kv = _lane_half((2 * blk, LANES))
    lo_q = _lane_half((blk, LANES))
    rowq = lax.broadcasted_iota(I32, (2 * blk, LANES), 0)
    q_keep = (rowq < blk) == lo_kv
    top_rows = lax.broadcasted_iota(I32, (2 * blk, 1), 0) < blk

    k0 = WB
    v0 = WB + WKV
    kk = jnp.concatenate([prev_ref[:, k0:k0 + WKV], cur_ref[:, k0:k0 + WKV]], axis=0).astype(F32)
    vv = jnp.concatenate([prev_ref[:, v0:v0 + WKV], cur_ref[:, v0:v0 + WKV]], axis=0).astype(F32)
    kk_sw = pltpu.roll(kk, LANES // 2, axis=1)
    vv_sw = pltpu.roll(vv, LANES // 2, axis=1)
    k_dup = (jnp.where(lo_kv, kk, kk_sw), jnp.where(lo_kv, kk_sw, kk))
    v_dup = (jnp.where(lo_kv, vv, vv_sw), jnp.where(lo_kv, vv_sw, vv))

    for p in range(H_B // 2):
        j = (2 * p) // (H_B // H_B_KV)
        qp = cur_ref[:, p * LANES:(p + 1) * LANES].astype(F32) * (DH_B ** -0.5)
        q_st = jnp.where(q_keep, jnp.concatenate([qp, qp], axis=0), 0.0)
        s = _mm_nt(q_st, k_dup[j])
        s = jnp.where(valid, s + bias_ref[p], -1e30)
        sink = jnp.where(top_rows, sink_ref[2 * p], sink_ref[2 * p + 1])
        m = jnp.maximum(jnp.max(s, axis=-1, keepdims=True), sink)
        pe = jnp.exp(s - m)
        den = jnp.sum(pe, axis=-1, keepdims=True) + jnp.exp(sink - m)
        o2 = _mm(pe, v_dup[j]) / den
        o_ref[:, p * LANES:(p + 1) * LANES] = jnp.where(lo_q, o2[0:blk], o2[blk:2 * blk]).astype(BF16)


def _swa(b1, sinks, bias_tab, seq):
    t, nc = b1.shape
    nbs = seq // ATT_BLK
    kern = functools.partial(_swa_kernel, blocks_per_seq=nbs)
    return pl.pallas_call(
        kern,
        grid=(t // ATT_BLK,),
        in_specs=[
            pl.BlockSpec(memory_space=pltpu.SMEM),
            pl.BlockSpec((ATT_BLK, nc), lambda i: (i, 0)),
            pl.BlockSpec((ATT_BLK, nc), lambda i: (jnp.maximum(i - 1, 0), 0)),
            pl.BlockSpec((H_B // 2, 2 * ATT_BLK, 2 * ATT_BLK), lambda i: (0, 0, 0)),
        ],
        out_specs=pl.BlockSpec((ATT_BLK, WB), lambda i: (i, 0)),
        out_shape=jax.ShapeDtypeStruct((t, WB), BF16),
        compiler_params=_cparams(("arbitrary",)),
        name="swa",
    )(sinks, b1, b1, bias_tab)


def _t5_bias_table(rel_bias):
    qi = jnp.arange(ATT_BLK)[:, None]
    ki = jnp.arange(2 * ATT_BLK)[None, :]
    n = jnp.maximum(qi + ATT_BLK - ki, 0)
    max_exact = NUM_BUCKETS // 2
    nf = jnp.maximum(n, 1).astype(F32)
    large = max_exact + (jnp.log(nf / max_exact) / math.log(MAX_DISTANCE / max_exact)
                         * (NUM_BUCKETS - max_exact)).astype(I32)
    bucket = jnp.where(n < max_exact, n, jnp.minimum(large, NUM_BUCKETS - 1))
    onehot = (bucket[..., None] == jnp.arange(NUM_BUCKETS)).astype(F32)
    bias = jnp.einsum('qkb,bh->hqk', onehot, rel_bias.astype(F32), precision=lax.Precision.HIGHEST)
    return bias.reshape(H_B // 2, 2 * ATT_BLK, 2 * ATT_BLK)


def _route_tail(x1, mod_ref, nw2_ref, rwhl_ref, rwhi_ref, rb_ref, tril_ref, carry_ref,
                h2_ref, meta_ref, metat_ref, cnt_ref):
    tm = x1.shape[0]
    h2 = _norm_mod(x1, nw2_ref[...], mod_ref[0, 4:5, :], mod_ref[0, 3:4, :])
    _store_row_tiles(h2_ref, h2)
    h_hi = h2.astype(BF16)
    h_lo = (h2 - h_hi.astype(F32)).astype(BF16)
    r = jnp.dot(h_hi, rwhl_ref[...], preferred_element_type=F32)
    logits = (r[:, 0:LANES] + r[:, LANES:2 * LANES]
              + jnp.dot(h_lo, rwhi_ref[...], preferred_element_type=F32) + rb_ref[...])
    lane = lax.broadcasted_iota(I32, (tm, LANES), 1)
    lane_f = lane.astype(F32)
    big = 1e9
    neg = -jnp.inf
    is_grp = (lane >= R_GRP0) & (lane < R_GRP0 + N_GROUPS)
    lg1 = jnp.where(is_grp, logits, neg)
    m1 = jnp.max(lg1, axis=-1, keepdims=True)
    grp = jnp.min(jnp.where(lg1 == m1, lane_f, big), axis=-1, keepdims=True) - R_GRP0
    p_top = 1.0 / jnp.sum(jnp.where(is_grp, jnp.exp(lg1 - m1), 0.0), axis=-1, keepdims=True)
    lane_grp = jnp.right_shift(lane, EXPERTS_PER_GROUP.bit_length() - 1).astype(F32)
    in_grp = (lane < N_EXPERTS) & (lane_grp == grp)
    l2 = jnp.where(in_grp, logits, neg)
    t1 = jnp.max(l2, axis=-1, keepdims=True)
    e1 = jnp.min(jnp.where(l2 == t1, lane_f, big), axis=-1, keepdims=True)
    l2b = jnp.where(lane_f == e1, neg, l2)
    t2 = jnp.max(l2b, axis=-1, keepdims=True)
    e2 = jnp.min(jnp.where(l2b == t2, lane_f, big), axis=-1, keepdims=True)
    ex = jnp.exp(t2 - t1)
    gate1 = p_top / (1.0 + ex)
    gate2 = p_top * ex / (1.0 + ex)
    hit1 = lane_f == e1
    hit2 = lane_f == e2
    onehot = jnp.where(hit1 | hit2, 1.0, 0.0)
    incl = jnp.dot(tril_ref[...], onehot.astype(BF16), preferred_element_type=F32)
    excl = incl - onehot + carry_ref[...]
    rank1 = jnp.sum(jnp.where(hit1, excl, 0.0), axis=-1, keepdims=True)
    rank2 = jnp.sum(jnp.where(hit2, excl, 0.0), axis=-1, keepdims=True)
    carry_ref[...] = carry_ref[...] + incl[tm - 1:tm, :]
    cnt_ref[...] = carry_ref[...]
    meta = jnp.zeros((tm, LANES), F32)
    for idx, val in ((M_E1, e1), (M_E2, e2), (M_R1, rank1), (M_R2, rank2), (M_G1, gate1), (M_G2, gate2)):
        meta = jnp.where(lane == idx, val, meta)
    meta_ref[...] = meta
    metat_ref[0] = meta.T[0:8, :]


def _chunk_rows(cidx, n):
    return pl.ds(cidx, n, stride=ROW_TILE)


def _store_row_tiles(ref, x):
    for cidx in range(ROW_TILE):
        ref[_chunk_rows(cidx, x.shape[0]), :] = x[:, cidx * LANES:(cidx + 1) * LANES]


def _load_row_tiles(ref):
    n = ref.shape[0] // ROW_TILE
    return jnp.concatenate([ref[_chunk_rows(cidx, n), :] for cidx in range(ROW_TILE)], axis=1)


def _tail_in_specs(d, tm, tps):
    return [
        pl.BlockSpec((1, 6, d), lambda i: (i // tps, 0, 0)),
        pl.BlockSpec((1, d), lambda i: (0, 0)),
        pl.BlockSpec((d, 2 * LANES), lambda i: (0, 0)),
        pl.BlockSpec((d, LANES), lambda i: (0, 0)),
        pl.BlockSpec((1, LANES), lambda i: (0, 0)),
        pl.BlockSpec((tm, tm), lambda i: (0, 0)),
    ]


def _tail_out(t, d, tm):
    specs = [
        pl.BlockSpec((tm, d), lambda i: (i, 0)),
        pl.BlockSpec((tm * ROW_TILE, LANES), lambda i: (i, 0)),
        pl.BlockSpec((tm, LANES), lambda i: (i, 0)),
        pl.BlockSpec((1, 8, tm), lambda i: (i, 0, 0)),
        pl.BlockSpec((1, LANES), lambda i: (0, 0)),
    ]
    shapes = [
        jax.ShapeDtypeStruct((t, d), F32),
        jax.ShapeDtypeStruct((t * ROW_TILE, LANES), F32),
        jax.ShapeDtypeStruct((t, LANES), F32),
        jax.ShapeDtypeStruct((t // tm, 8, tm), F32),
        jax.ShapeDtypeStruct((1, LANES), F32),
    ]
    return specs, shapes


def _outproj_kernel(oa_ref, ob_ref, x_ref, wo_ref, mod_ref, nw2_ref, rwhl_ref, rwhi_ref, rb_ref,
                    tril_ref, x1_ref, h2_ref, meta_ref, metat_ref, cnt_ref, carry_ref):
    @pl.when(pl.program_id(0) == 0)
    def _():
        carry_ref[...] = jnp.zeros(carry_ref.shape, F32)

    y = (jnp.dot(oa_ref[...], wo_ref[0:WA, :], preferred_element_type=F32)
         + jnp.dot(ob_ref[...], wo_ref[WA:WA + WB, :], preferred_element_type=F32))
    x1 = x_ref[...] + mod_ref[0, 2:3, :] * y
    x1_ref[...] = x1
    _route_tail(x1, mod_ref, nw2_ref, rwhl_ref, rwhi_ref, rb_ref, tril_ref, carry_ref,
                h2_ref, meta_ref, metat_ref, cnt_ref)


def _outproj_route(oa, ob, x, w_out_b, mod, nw2, rwhl, rwhi, rb, tril, seq, tm):
    t, d = x.shape
    tps = seq // tm
    out_specs, out_shapes = _tail_out(t, d, tm)
    return pl.pallas_call(
        _outproj_kernel,
        grid=(t // tm,),
        in_specs=[
            pl.BlockSpec((tm, WA), lambda i: (i, 0)),
            pl.BlockSpec((tm, WB), lambda i: (i, 0)),
            pl.BlockSpec((tm, d), lambda i: (i, 0)),
            pl.BlockSpec((WA + WB, d), lambda i: (0, 0)),
        ] + _tail_in_specs(d, tm, tps),
        out_specs=out_specs,
        out_shape=out_shapes,
        scratch_shapes=[pltpu.VMEM((1, LANES), F32)],
        compiler_params=_cparams(("arbitrary",)),
        name="outproj_route",
    )(oa, ob, x, w_out_b, mod, nw2, rwhl, rwhi, rb, tril)


def _pool_kernel(x_ref, nw1_ref, pw_ref, ps_ref, mod_ref, nw2_ref, rwhl_ref, rwhi_ref, rb_ref,
                 tril_ref, x1_ref, h2_ref, meta_ref, metat_ref, cnt_ref, carry_ref, e1_ref, e2_ref, e3_ref,
                 *, blocks_per_seq):
    tp, d = x_ref.shape
    hist = POOL_HIST
    i = pl.program_id(0)

    @pl.when(i == 0)
    def _():
        carry_ref[...] = jnp.zeros(carry_ref.shape, F32)

    @pl.when(i % blocks_per_seq == 0)
    def _():
        e1_ref[0:hist, :] = jnp.zeros((hist, d), F32)

    x = x_ref[...]
    h = _norm_mod(x, nw1_ref[...], mod_ref[0, 1:2, :], mod_ref[0, 0:1, :])
    e1_ref[hist:hist + tp, :] = h
    pos = (i % blocks_per_seq) * tp + lax.broadcasted_iota(I32, (tp, 1), 0)
    gdim = POOL_GROUP_DIM
    ys = []
    for gi, win in enumerate(POOL_WINDOWS):
        c0 = gi * gdim
        levels = win.bit_length() - 1
        src, spare = e1_ref, (e2_ref, e3_ref)
        for m in range(1, levels + 1):
            shift = 1 << (m - 1)
            start = hist - 8 * (levels - m)
            n = hist + tp - start
            cur = src[start:start + n, c0:c0 + gdim] + src[start - shift:start - shift + n, c0:c0 + gdim]
            if m < levels:
                dst = spare[m % 2]
                dst[start:start + n, c0:c0 + gdim] = cur
                src = dst
        cnt = jnp.minimum(pos + 1, win).astype(F32)
        pooled = cur / cnt - h[:, c0:c0 + gdim]
        ys.append(_mm(pooled, pw_ref[gi]))
    e1_ref[0:hist, :] = h[tp - hist:tp, :]
    y = jnp.concatenate(ys, axis=1) * ps_ref[...]
    x1 = x + mod_ref[0, 2:3, :] * y
    x1_ref[...] = x1
    _route_tail(x1, mod_ref, nw2_ref, rwhl_ref, rwhi_ref, rb_ref, tril_ref, carry_ref,
                h2_ref, meta_ref, metat_ref, cnt_ref)


def _pool_route(x, nw1, pool_w_b, pool_scale, mod, nw2, rwhl, rwhi, rb, tril, seq, tp):
    t, d = x.shape
    tps = seq // tp
    out_specs, out_shapes = _tail_out(t, d, tp)
    kern = functools.partial(_pool_kernel, blocks_per_seq=tps)
    return pl.pallas_call(
        kern,
        grid=(t // tp,),
        in_specs=[
            pl.BlockSpec((tp, d), lambda i: (i, 0)),
            pl.BlockSpec((1, d), lambda i: (0, 0)),
            pl.BlockSpec((4, POOL_GROUP_DIM, POOL_GROUP_DIM), lambda i: (0, 0, 0)),
            pl.BlockSpec((1, d), lambda i: (0, 0)),
        ] + _tail_in_specs(d, tp, tps),
        out_specs=out_specs,
        out_shape=out_shapes,
        scratch_shapes=[
            pltpu.VMEM((1, LANES), F32),
            pltpu.VMEM((tp + POOL_HIST, d), F32),
            pltpu.VMEM((tp + POOL_HIST, d), F32),
            pltpu.VMEM((tp + POOL_HIST, d), F32),
        ],
        compiler_params=_cparams(("arbitrary",)),
        name="pool_route",
    )(x, nw1, pool_w_b, pool_scale, mod, nw2, rwhl, rwhi, rb, tril)


def _dispatch_kernel(dest_ref, fill_ref, h_ref, xs_ref, zero_ref, sem):
    td = h_ref.shape[0] // ROW_TILE

    def row_tile(r, n=1):
        return pl.ds(pl.multiple_of(r * ROW_TILE, ROW_TILE), n * ROW_TILE)

    @pl.when(pl.program_id(0) == 0)
    def _():
        zero_ref[...] = jnp.zeros(zero_ref.shape, zero_ref.dtype)

        def fill(e, carry):
            pos = fill_ref[0, e]
            n = fill_ref[1, e]
            for bit in reversed(range(MOE_ROWS.bit_length() - 1)):
                size = 1 << bit
                take = (n & size) != 0

                @pl.when(take)
                def _():
                    pltpu.make_async_copy(zero_ref.at[row_tile(0, size)], xs_ref.at[row_tile(pos, size)], sem).start()

                pos = pos + jnp.where(take, size, 0)
            return carry

        lax.fori_loop(0, N_EXPERTS, fill, 0)

        def drain_fill(r, carry):
            pltpu.make_async_copy(zero_ref.at[row_tile(0)], xs_ref.at[row_tile(0)], sem).wait()
            return carry

        lax.fori_loop(0, fill_ref[1, N_EXPERTS], drain_fill, 0)

    def issue(tk, carry):
        for s in range(2):
            d = dest_ref[0, 0, s * td + tk]
            pltpu.make_async_copy(h_ref.at[row_tile(tk)], xs_ref.at[row_tile(d)], sem).start(priority=s)
        return carry

    lax.fori_loop(0, td, issue, 0)

    def drain(tk, carry):
        pltpu.make_async_copy(h_ref.at[row_tile(0)], xs_ref.at[row_tile(0)], sem).wait()
        return carry

    lax.fori_loop(0, 2 * td, drain, 0)


def _dispatch(dest3, fill, h2, n_rows, td):
    t = h2.shape[0] // ROW_TILE
    return pl.pallas_call(
        _dispatch_kernel,
        grid=(t // td,),
        in_specs=[
            pl.BlockSpec((1, 1, 2 * td), lambda i: (i, 0, 0), memory_space=pltpu.SMEM),
            pl.BlockSpec(memory_space=pltpu.SMEM),
            pl.BlockSpec((td * ROW_TILE, LANES), lambda i: (i, 0)),
        ],
        out_specs=pl.BlockSpec(memory_space=pl.ANY),
        out_shape=jax.ShapeDtypeStruct((n_rows * ROW_TILE, LANES), h2.dtype),
        scratch_shapes=[pltpu.VMEM((MOE_ROWS // 2 * ROW_TILE, LANES), h2.dtype), pltpu.SemaphoreType.DMA(())],
        compiler_params=_cparams(("arbitrary",)),
        name="moe_dispatch",
    )(dest3, fill, h2)


def _expert_kernel(be_ref, nu_ref, xs_ref, wg_ref, wu_ref, wd_ref, y_ref, wgb_ref, wub_ref, wdb_ref):
    i = pl.program_id(0)
    prev = be_ref[jnp.maximum(i - 1, 0)]
    changed = (i == 0) | (be_ref[i] != prev)

    @pl.when(changed)
    def _():
        wgb_ref[...] = wg_ref[0, 0].astype(BF16)
        wub_ref[...] = wu_ref[0, 0].astype(BF16)
        wdb_ref[...] = wd_ref[0, 0].astype(BF16)

    @pl.when(i < nu_ref[0])
    def _():
        xb = _load_row_tiles(xs_ref).astype(BF16)
        g = jnp.dot(xb, wgb_ref[...], preferred_element_type=F32)
        u = jnp.dot(xb, wub_ref[...], preferred_element_type=F32)
        a = (_silu(g) * u).astype(BF16)
        _store_row_tiles(y_ref, jnp.dot(a, wdb_ref[...], preferred_element_type=F32))

    @pl.when(i >= nu_ref[0])
    def _():
        y_ref[...] = jnp.zeros(y_ref.shape, F32)


def _experts(blk_expert, n_used, xs, w_gate, w_up, w_down, layer):
    n_rows = xs.shape[0] // ROW_TILE
    d = ROW_TILE * LANES
    blk = (MOE_ROWS * ROW_TILE, LANES)
    n_blk = n_rows // MOE_ROWS
    return pl.pallas_call(
        _expert_kernel,
        grid_spec=pltpu.PrefetchScalarGridSpec(
            num_scalar_prefetch=2,
            grid=(n_blk,),
            in_specs=[
                pl.BlockSpec(blk, lambda i, be, nu: (i, 0)),
                pl.BlockSpec((1, 1, d, D_FF), lambda i, be, nu: (layer, be[i], 0, 0)),
                pl.BlockSpec((1, 1, d, D_FF), lambda i, be, nu: (layer, be[i], 0, 0)),
                pl.BlockSpec((1, 1, D_FF, d), lambda i, be, nu: (layer, be[i], 0, 0)),
            ],
            out_specs=pl.BlockSpec(blk, lambda i, be, nu: (i, 0)),
            scratch_shapes=[
                pltpu.VMEM((d, D_FF), BF16),
                pltpu.VMEM((d, D_FF), BF16),
                pltpu.VMEM((D_FF, d), BF16),
            ],
        ),
        out_shape=jax.ShapeDtypeStruct(xs.shape, F32),
        compiler_params=_cparams(("arbitrary",)),
        name="moe_experts",
    )(blk_expert, n_used, xs, w_gate, w_up, w_down)


def _combine_kernel(dest_ref, ys_ref, x1_ref, meta_ref, mod_ref, fw_ref, o_ref, buf_ref, sem, *, final_norm):
    tc = x1_ref.shape[0]

    def row_tile(r):
        return pl.ds(pl.multiple_of(r * ROW_TILE, ROW_TILE), ROW_TILE)

    def issue(tk, carry):
        for s in range(2):
            d = dest_ref[0, 0, s * tc + tk]
            pltpu.make_async_copy(ys_ref.at[row_tile(d)], buf_ref.at[s, row_tile(tk)], sem).start(priority=s)
        return carry

    lax.fori_loop(0, tc, issue, 0)

    def drain(tk, carry):
        pltpu.make_async_copy(ys_ref.at[row_tile(0)], buf_ref.at[0, row_tile(0)], sem).wait()
        return carry

    lax.fori_loop(0, 2 * tc, drain, 0)

    meta = meta_ref[...]
    g1 = meta[:, M_G1:M_G1 + 1]
    g2 = meta[:, M_G2:M_G2 + 1]
    n_chunks = ROW_TILE
    ssq = jnp.zeros((tc, 1), F32)
    for cidx in range(n_chunks):
        sl = slice(cidx * LANES, (cidx + 1) * LANES)
        rows = _chunk_rows(cidx, tc)
        moe = buf_ref[0, rows, :] * g1 + buf_ref[1, rows, :] * g2
        x2 = x1_ref[:, sl] + mod_ref[0, 5:6, sl] * moe
        o_ref[:, sl] = x2
        if final_norm:
            ssq = ssq + jnp.sum(x2 * x2, axis=-1, keepdims=True)
    if final_norm:
        inv = lax.rsqrt(ssq * (1.0 / (n_chunks * LANES)) + EPS)
        for cidx in range(n_chunks):
            sl = slice(cidx * LANES, (cidx + 1) * LANES)
            o_ref[:, sl] = o_ref[:, sl] * inv * fw_ref[:, sl]


def _combine(dest3, ys, x1, meta, mod, fw, seq, tc, final_norm):
    t, d = x1.shape
    tps = seq // tc
    kern = functools.partial(_combine_kernel, final_norm=final_norm)
    return pl.pallas_call(
        kern,
        grid=(t // tc,),
        in_specs=[
            pl.BlockSpec((1, 1, 2 * tc), lambda i: (i, 0, 0), memory_space=pltpu.SMEM),
            pl.BlockSpec(memory_space=pl.ANY),
            pl.BlockSpec((tc, d), lambda i: (i, 0)),
            pl.BlockSpec((tc, LANES), lambda i: (i, 0)),
            pl.BlockSpec((1, 6, d), lambda i: (i // tps, 0, 0)),
            pl.BlockSpec((1, d), lambda i: (0, 0)),
        ],
        out_specs=pl.BlockSpec((tc, d), lambda i: (i, 0)),
        out_shape=jax.ShapeDtypeStruct((t, d), F32),
        scratch_shapes=[pltpu.VMEM((2, tc * ROW_TILE, LANES), F32), pltpu.SemaphoreType.DMA(())],
        compiler_params=_cparams(("arbitrary",)),
        name="moe_combine",
    )(dest3, ys, x1, meta, mod, fw)


def _moe(x1, h2, meta, metat, cnt, mod, fw, w_gate, w_up, w_down, layer, seq, tile, final_norm):
    t, d = x1.shape
    n_rows = -(-(2 * t + N_EXPERTS * (MOE_ROWS - 1)) // MOE_ROWS) * MOE_ROWS
    n_blk = n_rows // MOE_ROWS
    counts = cnt[0, :N_EXPERTS].astype(I32)
    padded = (counts + MOE_ROWS - 1) // MOE_ROWS * MOE_ROWS
    pad_end = jnp.cumsum(padded)
    pad_start = pad_end - padded
    expert = metat[:, M_E1:M_E2 + 1, :].astype(I32)
    rank = metat[:, M_R1:M_R2 + 1, :].astype(I32)
    start_of = jnp.sum(jnp.where(expert[..., None] == jnp.arange(N_EXPERTS, dtype=I32), pad_start, 0), axis=-1)
    dest3 = (start_of + rank).reshape(t // tile, 1, 2 * tile)
    blk_row0 = jnp.arange(n_blk, dtype=I32) * MOE_ROWS
    blk_expert = jnp.minimum(jnp.sum((pad_end[None, :] <= blk_row0[:, None]).astype(I32), axis=1),
                             N_EXPERTS - 1)
    n_used = (pad_end[-1:] // MOE_ROWS).astype(I32)
    n_fill = padded - counts
    fill = jnp.stack([jnp.concatenate([pad_start + counts, jnp.zeros((1,), I32)]),
                      jnp.concatenate([n_fill, jnp.sum(n_fill, keepdims=True)])])
    xs = _dispatch(dest3, fill, h2, n_rows, tile)
    ys = _experts(blk_expert, n_used, xs, w_gate, w_up, w_down, layer)
    return _combine(dest3, ys, x1, meta, mod, fw, seq, tile, final_norm)


def _router_params(r1_w, r1_b, r2_w, r2_b):
    d = r1_w.shape[0]
    rw = jnp.zeros((d, LANES), F32).at[:, :N_EXPERTS].set(r2_w).at[:, R_GRP0:R_GRP0 + N_GROUPS].set(r1_w)
    rb = jnp.zeros((1, LANES), F32).at[0, :N_EXPERTS].set(r2_b).at[0, R_GRP0:R_GRP0 + N_GROUPS].set(r1_b)
    hi = rw.astype(BF16)
    lo = (rw - hi.astype(F32)).astype(BF16)
    return jnp.concatenate([hi, lo], axis=1), hi, rb


def kernel(x, c, ada_w, ada_b, norm_mix_w, norm_ffn_w, ab_w_in, ab_conv_w, ab_a_log, ab_dt_bias, ab_onorm_w, ab_sinks, rel_bias, ab_w_out, pool_w, pool_scale, r1_w, r1_b, r2_w, r2_b, moe_w_gate, moe_w_up, moe_w_down, final_norm_w):
    bsz, seq, d = x.shape
    t = bsz * seq
    tile = min(512, seq)
    lb = min(256, seq)
    xf = x.reshape(t, d)

    mod_all = _ada(c, ada_w, ada_b).reshape(DEPTH, bsz, 6, d)
    tril = jnp.tril(jnp.ones((tile, tile), BF16))
    fw = final_norm_w.reshape(1, d)

    w_in = ab_w_in[0]
    cuts = (0, WA, 2 * WA, 3 * WA, 4 * WA, 4 * WA + H_A, 4 * WA + 2 * H_A)
    qkvz = w_in[:, :cuts[4]]
    ba = w_in[:, cuts[4]:cuts[6]]
    qkv_b = w_in[:, cuts[6]:]
    w_all = jnp.concatenate([qkvz, qkv_b, ba, jnp.zeros((d, LANES - 2 * H_A), F32)], axis=1).astype(BF16)
    a1, b1, g1 = _inproj(xf, mod_all[0], norm_mix_w[0:1], w_all, seq, tile)

    pad = jnp.zeros((LANES - 2 * H_A,), F32)
    alog_row = jnp.concatenate([jnp.zeros((H_A,), F32), ab_a_log[0], pad]).reshape(1, LANES)
    dt_row = jnp.concatenate([jnp.zeros((H_A,), F32), ab_dt_bias[0], pad]).reshape(1, LANES)
    onw_row = jnp.concatenate([ab_onorm_w[0], ab_onorm_w[0]]).reshape(1, LANES)
    oa = _deltanet(a1, g1, ab_conv_w[0], alog_row, dt_row, onw_row, seq, lb)
    ob = _swa(b1, ab_sinks[0], _t5_bias_table(rel_bias), seq)

    rwhl, rwhi, rb = _router_params(r1_w[0], r1_b[0], r2_w[0], r2_b[0])
    x1, h2, meta, metat, cnt = _outproj_route(oa, ob, xf, ab_w_out[0].astype(BF16), mod_all[0], norm_ffn_w[0:1],
                                       rwhl, rwhi, rb, tril, seq, tile)
    x2 = _moe(x1, h2, meta, metat, cnt, mod_all[0], fw, moe_w_gate, moe_w_up, moe_w_down, 0,
              seq, tile, False)

    rwhl, rwhi, rb = _router_params(r1_w[1], r1_b[1], r2_w[1], r2_b[1])
    x3, h2, meta, metat, cnt = _pool_route(x2, norm_mix_w[1:2], pool_w[0].astype(BF16), pool_scale[0].reshape(1, d),
                                    mod_all[1], norm_ffn_w[1:2], rwhl, rwhi, rb, tril, seq, tile)
    out = _moe(x3, h2, meta, metat, cnt, mod_all[1], fw, moe_w_gate, moe_w_up, moe_w_down, 1,
               seq, tile, True)
    return out.reshape(bsz, seq, d)
```

```python
import functools
import math

import jax
import jax.numpy as jnp
from jax import lax
from jax.experimental import pallas as pl
from jax.experimental.pallas import tpu as pltpu

F32 = jnp.float32
BF16 = jnp.bfloat16
I32 = jnp.int32

D_MODEL = 1024
DEPTH = 2
H_A = 8
DK_A = 64
CONV_K = 4
CHUNK = 64
H_B = 8
H_B_KV = 2
DH_B = 64
WINDOW = 128
ATT_BLK = 128
NUM_BUCKETS = 32
MAX_DISTANCE = 128
POOL_WINDOWS = (2, 4, 8, 16)
POOL_GROUP_DIM = D_MODEL // 4
POOL_HIST = 32
N_GROUPS = 4
EXPERTS_PER_GROUP = 8
N_EXPERTS = N_GROUPS * EXPERTS_PER_GROUP
D_FF = D_MODEL // 2
EPS = 1e-6

WA = H_A * DK_A
WB = H_B * DH_B
WKV = H_B_KV * DH_B
N_PAIR = H_A // 2
LANES = 128
ROW_TILE = D_MODEL // LANES
MOE_ROWS = 512
VMEM_LIMIT = 56 * 1024 * 1024

M_E1, M_E2, M_R1, M_R2, M_G1, M_G2 = 0, 1, 2, 3, 4, 5
R_GRP0 = N_EXPERTS


def _cparams(sem):
    return pltpu.CompilerParams(dimension_semantics=sem, vmem_limit_bytes=VMEM_LIMIT)


def _mm(a, b):
    return jnp.dot(a.astype(BF16), b.astype(BF16), preferred_element_type=F32)


def _mm_nt(a, b):
    return lax.dot_general(a.astype(BF16), b.astype(BF16), (((1,), (1,)), ((), ())),
                           preferred_element_type=F32)


def _mm_x3(a, b):
    ah = a.astype(BF16)
    al = (a - ah.astype(F32)).astype(BF16)
    bh = b.astype(BF16)
    bl = (b - bh.astype(F32)).astype(BF16)
    return jnp.dot(jnp.concatenate([ah, al, ah], axis=1), jnp.concatenate([bh, bh, bl], axis=0),
                   preferred_element_type=F32)


def _split3(x):
    hi = x.astype(BF16)
    r = x - hi.astype(F32)
    mid = r.astype(BF16)
    lo = (r - mid.astype(F32)).astype(BF16)
    return hi, mid, lo


def _norm_mod(x, nw, sc, sh):
    ms = jnp.mean(x * x, axis=-1, keepdims=True)
    return (x * lax.rsqrt(ms + EPS) * nw) * (1.0 + sc) + sh


def _silu(x):
    return x * jax.nn.sigmoid(x)


def _ada_kernel(ct_ref, w_ref, b_ref, o_ref):
    nb = ct_ref.shape[1]
    d = ct_ref.shape[0]
    rows = []
    for b in range(nb):
        acc = None
        for k0 in range(0, d, LANES):
            ct = ct_ref[k0:k0 + LANES, b:b + 1]
            part = jnp.sum(_silu(ct) * w_ref[0, k0:k0 + LANES, :], axis=0, keepdims=True)
            acc = part if acc is None else acc + part
        rows.append(acc)
    o_ref[0] = jnp.concatenate(rows, axis=0) + b_ref[0]


def _ada(c, ada_w, ada_b):
    nb, d = c.shape
    depth, _, n6 = ada_w.shape
    cols = 1536
    return pl.pallas_call(
        _ada_kernel,
        grid=(depth, n6 // cols),
        in_specs=[
            pl.BlockSpec((d, nb), lambda l, j: (0, 0)),
            pl.BlockSpec((1, d, cols), lambda l, j: (l, 0, j)),
            pl.BlockSpec((1, 1, cols), lambda l, j: (l, 0, j)),
        ],
        out_specs=pl.BlockSpec((1, nb, cols), lambda l, j: (l, 0, j)),
        out_shape=jax.ShapeDtypeStruct((depth, nb, n6), F32),
        compiler_params=_cparams(("arbitrary", "arbitrary")),
        name="ada_mod",
    )(c.T, ada_w, ada_b.reshape(depth, 1, n6))


def _inproj_kernel(x_ref, mod_ref, nw_ref, w_ref, a_ref, b_ref, g_ref):
    h = _norm_mod(x_ref[...], nw_ref[...], mod_ref[0, 1:2, :], mod_ref[0, 0:1, :])
    hb = h.astype(BF16)
    na = a_ref.shape[1]
    nbb = b_ref.shape[1]
    a_ref[...] = jnp.dot(hb, w_ref[:, 0:na], preferred_element_type=F32)
    b_ref[...] = jnp.dot(hb, w_ref[:, na:na + nbb], preferred_element_type=F32).astype(BF16)
    g_ref[...] = jnp.dot(hb, w_ref[:, na + nbb:], preferred_element_type=F32)


def _inproj(x, mod, nw, w_all, seq, tm):
    t, d = x.shape
    tps = seq // tm
    na, nbb, ng = 4 * WA, WB + 2 * WKV, LANES
    return pl.pallas_call(
        _inproj_kernel,
        grid=(t // tm,),
        in_specs=[
            pl.BlockSpec((tm, d), lambda i: (i, 0)),
            pl.BlockSpec((1, 6, d), lambda i: (i // tps, 0, 0)),
            pl.BlockSpec((1, d), lambda i: (0, 0)),
            pl.BlockSpec((d, na + nbb + ng), lambda i: (0, 0)),
        ],
        out_specs=[
            pl.BlockSpec((tm, na), lambda i: (i, 0)),
            pl.BlockSpec((tm, nbb), lambda i: (i, 0)),
            pl.BlockSpec((tm, ng), lambda i: (i, 0)),
        ],
        out_shape=[
            jax.ShapeDtypeStruct((t, na), F32),
            jax.ShapeDtypeStruct((t, nbb), BF16),
            jax.ShapeDtypeStruct((t, ng), F32),
        ],
        compiler_params=_cparams(("arbitrary",)),
        name="inproj",
    )(x, mod, nw, w_all)


def _lane_half(shape):
    return lax.broadcasted_iota(I32, shape, 1) < (LANES // 2)


def _half_sums(x2, lo):
    se = jnp.sum(jnp.where(lo, x2, 0.0), axis=-1, keepdims=True)
    so = jnp.sum(jnp.where(lo, 0.0, x2), axis=-1, keepdims=True)
    return jnp.where(lo, se, so)


def _deltanet_kernel(a_ref, gp_ref, cw_ref, alog_ref, dt_ref, onw_ref, o_ref,
                     ext_ref, s_ref, q_ref, k_ref, v_ref, gf_ref, bf_ref, *, blocks_per_seq):
    lb = a_ref.shape[0]
    i = pl.program_id(0)

    @pl.when(i % blocks_per_seq == 0)
    def _():
        ext_ref[0:8, :] = jnp.zeros((8, ext_ref.shape[1]), F32)
        s_ref[...] = jnp.zeros(s_ref.shape, F32)

    lo_b = _lane_half((lb, LANES))
    for part, dst in ((0, q_ref), (1, k_ref), (2, v_ref)):
        for p in range(N_PAIR):
            c0 = part * WA + p * LANES
            xin = a_ref[:, c0:c0 + LANES]
            ext_ref[8:8 + lb, c0:c0 + LANES] = xin
            acc = cw_ref[CONV_K - 1:CONV_K, c0:c0 + LANES] * xin
            for j in range(CONV_K - 1):
                acc = acc + cw_ref[j:j + 1, c0:c0 + LANES] * ext_ref[pl.ds(8 - (CONV_K - 1) + j, lb), c0:c0 + LANES]
            ext_ref[0:8, c0:c0 + LANES] = xin[lb - 8:lb, :]
            y = _silu(acc)
            if part < 2:
                y = y * lax.rsqrt(_half_sums(y * y, lo_b) + EPS)
                if part == 0:
                    y = y * (DK_A ** -0.5)
            dst[:, p * LANES:(p + 1) * LANES] = y

    gp = gp_ref[...]
    bf_ref[...] = jax.nn.sigmoid(gp)
    xg = gp + dt_ref[...]
    softplus = jnp.maximum(xg, 0.0) + jnp.log(1.0 + jnp.exp(-jnp.abs(xg)))
    gf_ref[...] = -jnp.exp(alog_ref[...]) * softplus

    c = CHUNK
    c2 = 2 * c
    row = lax.broadcasted_iota(I32, (c2, c2), 0)
    col = lax.broadcasted_iota(I32, (c2, c2), 1)
    same_blk = (row < c) == (col < c)
    incl = same_blk & (row >= col)
    strict = same_blk & (row > col)
    eye = row == col
    eye_f = jnp.where(eye, 1.0, 0.0)
    lo = _lane_half((c, LANES))

    def stack(x):
        return jnp.where(same_blk, jnp.concatenate([x, x], axis=0), 0.0)

    def unstack(x):
        return jnp.where(lo, x[0:c], x[c:c2])

    rb = lax.broadcasted_iota(I32, (lb, lb), 0)
    cb = lax.broadcasted_iota(I32, (lb, lb), 1)
    shift = CHUNK.bit_length() - 1
    tril_chunks = jnp.where((jnp.right_shift(rb, shift) == jnp.right_shift(cb, shift)) & (rb >= cb),
                            1.0, 0.0).astype(BF16)
    g_hi, g_mid, g_lo = _split3(gf_ref[...])
    gf_ref[...] = (jnp.dot(tril_chunks, g_hi, preferred_element_type=F32)
                   + jnp.dot(tril_chunks, g_mid, preferred_element_type=F32)
                   + jnp.dot(tril_chunks, g_lo, preferred_element_type=F32))

    pairs = range(N_PAIR)

    def chunk_body(ci, carry):
        r0 = pl.multiple_of(ci * c, c)
        gcc = gf_ref[pl.ds(r0, c), :]
        bfc = bf_ref[pl.ds(r0, c), :]
        q = [q_ref[pl.ds(r0, c), p * LANES:(p + 1) * LANES] for p in pairs]
        k = [k_ref[pl.ds(r0, c), p * LANES:(p + 1) * LANES] for p in pairs]
        v = [v_ref[pl.ds(r0, c), p * LANES:(p + 1) * LANES] for p in pairs]

        def lane_bcast(x, j):
            return jnp.broadcast_to(x[:, j:j + 1], (c, LANES))

        beta = [jnp.where(lo, lane_bcast(bfc, 2 * p), lane_bcast(bfc, 2 * p + 1)) for p in pairs]
        gce = [lane_bcast(gcc, H_A + 2 * p) for p in pairs]
        gco = [lane_bcast(gcc, H_A + 2 * p + 1) for p in pairs]
        gcol = [jnp.concatenate([gce[p], gco[p]], axis=0) for p in pairs]
        gc = [jnp.where(lo, gce[p], gco[p]) for p in pairs]
        grow = [jnp.sum(jnp.where(eye, gcol[p], 0.0), axis=0, keepdims=True) for p in pairs]
        decay = [jnp.where(incl, jnp.exp(jnp.where(incl, gcol[p] - grow[p], 0.0)), 0.0) for p in pairs]
        egc = [jnp.exp(gc[p]) for p in pairs]
        kb = [k[p] * beta[p] for p in pairs]
        k_st = [stack(k[p]) for p in pairs]
        n_mat = [jnp.where(strict, _mm_nt(stack(kb[p]), k_st[p]) * decay[p], 0.0) for p in pairs]
        a_qk = [_mm_nt(stack(q[p]), k_st[p]) * decay[p] for p in pairs]
        x_inv = [eye_f - n_mat[p] for p in pairs]
        y_pow = [_mm_x3(n_mat[p], n_mat[p]) for p in pairs]
        for it in range(5):
            x_inv = [x_inv[p] + _mm_x3(x_inv[p], y_pow[p]) for p in pairs]
            if it < 4:
                y_pow = [_mm_x3(y_pow[p], y_pow[p]) for p in pairs]
        sol = []
        for p in pairs:
            vb = v[p] * beta[p]
            kbg = kb[p] * egc[p]
            rhs = jnp.concatenate([jnp.concatenate([vb, vb], axis=0),
                                   jnp.concatenate([kbg, kbg], axis=0)], axis=1)
            sol.append(_mm(x_inv[p], rhs))
        u = [unstack(sol[p][:, 0:LANES]) for p in pairs]
        w = [unstack(sol[p][:, LANES:2 * LANES]) for p in pairs]
        state = [s_ref[p] for p in pairs]
        wq = [_mm(jnp.concatenate([w[p], q[p] * egc[p]], axis=0), state[p]) for p in pairs]
        v_new = [u[p] - wq[p][0:c] for p in pairs]
        av = [_mm(a_qk[p], jnp.concatenate([v_new[p], v_new[p]], axis=0)) for p in pairs]
        for p in pairs:
            k_dec = k[p] * jnp.exp(gc[p][c - 1:c, :] - gc[p])
            g_tot = jnp.where(row < c, jnp.exp(gcol[p][c - 1:c, :]), jnp.exp(gcol[p][c2 - 1:c2, :]))
            s_ref[p] = state[p] * g_tot + jnp.where(same_blk, _mm(k_dec.T, v_new[p]), 0.0)
        for p in pairs:
            out = wq[p][c:c2] + unstack(av[p])
            ms = _half_sums(out * out, lo) * (1.0 / DK_A)
            z = a_ref[pl.ds(r0, c), 3 * WA + p * LANES:3 * WA + (p + 1) * LANES]
            res = out * lax.rsqrt(ms + EPS) * onw_ref[...] * _silu(z)
            o_ref[pl.ds(r0, c), p * LANES:(p + 1) * LANES] = res.astype(BF16)
        return carry

    lax.fori_loop(0, lb // c, chunk_body, 0, unroll=True)


def _deltanet(a1, g1, conv_w, alog_row, dt_row, onw_row, seq, lb):
    t = a1.shape[0]
    kern = functools.partial(_deltanet_kernel, blocks_per_seq=seq // lb)
    return pl.pallas_call(
        kern,
        grid=(t // lb,),
        in_specs=[
            pl.BlockSpec((lb, 4 * WA), lambda i: (i, 0)),
            pl.BlockSpec((lb, LANES), lambda i: (i, 0)),
            pl.BlockSpec((CONV_K, 3 * WA), lambda i: (0, 0)),
            pl.BlockSpec((1, LANES), lambda i: (0, 0)),
            pl.BlockSpec((1, LANES), lambda i: (0, 0)),
            pl.BlockSpec((1, LANES), lambda i: (0, 0)),
        ],
        out_specs=pl.BlockSpec((lb, WA), lambda i: (i, 0)),
        out_shape=jax.ShapeDtypeStruct((t, WA), BF16),
        scratch_shapes=[
            pltpu.VMEM((lb + 8, 3 * WA), F32),
            pltpu.VMEM((N_PAIR, LANES, LANES), F32),
            pltpu.VMEM((lb, WA), F32),
            pltpu.VMEM((lb, WA), F32),
            pltpu.VMEM((lb, WA), F32),
            pltpu.VMEM((lb, LANES), F32),
            pltpu.VMEM((lb, LANES), F32),
        ],
        compiler_params=_cparams(("arbitrary",)),
        name="deltanet",
    )(a1, g1, conv_w, alog_row, dt_row, onw_row)


def _swa_kernel(sink_ref, cur_ref, prev_ref, bias_ref, o_ref, *, blocks_per_seq):
    blk = ATT_BLK
    i = pl.program_id(0)
    first = (i % blocks_per_seq) == 0
    row = lax.broadcasted_iota(I32, (2 * blk, 2 * blk), 0)
    col = lax.broadcasted_iota(I32, (2 * blk, 2 * blk), 1)
    qi = jnp.where(row < blk, row, row - blk)
    dist = qi + blk - col
    first_key = jnp.where(first, blk, 0)
    valid = (dist >= 0) & (dist < WINDOW) & (col >= first_key)
    lo_kv = _lane_half((2 * blk, LANES))
    lo_q = _lane_half((blk, LANES))
    rowq = lax.broadcasted_iota(I32, (2 * blk, LANES), 0)
    q_keep = (rowq < blk) == lo_kv
    top_rows = lax.broadcasted_iota(I32, (2 * blk, 1), 0) < blk

    k0 = WB
    v0 = WB + WKV
    kk = jnp.concatenate([prev_ref[:, k0:k0 + WKV], cur_ref[:, k0:k0 + WKV]], axis=0).astype(F32)
    vv = jnp.concatenate([prev_ref[:, v0:v0 + WKV], cur_ref[:, v0:v0 + WKV]], axis=0).astype(F32)
    kk_sw = pltpu.roll(kk, LANES // 2, axis=1)
    vv_sw = pltpu.roll(vv, LANES // 2, axis=1)
    k_dup = (jnp.where(lo_kv, kk, kk_sw), jnp.where(lo_kv, kk_sw, kk))
    v_dup = (jnp.where(lo_kv, vv, vv_sw), jnp.where(lo_kv, vv_sw, vv))

    for p in range(H_B // 2):
        j = (2 * p) // (H_B // H_B_KV)
        qp = cur_ref[:, p * LANES:(p + 1) * LANES].astype(F32) * (DH_B ** -0.5)
        q_st = jnp.where(q_keep, jnp.concatenate([qp, qp], axis=0), 0.0)
        s = _mm_nt(q_st, k_dup[j])
        s = jnp.where(valid, s + bias_ref[p], -1e30)
        sink = jnp.where(top_rows, sink_ref[2 * p], sink_ref[2 * p + 1])
        m = jnp.maximum(jnp.max(s, axis=-1, keepdims=True), sink)
        pe = jnp.exp(s - m)
        den = jnp.sum(pe, axis=-1, keepdims=True) + jnp.exp(sink - m)
        o2 = _mm(pe, v_dup[j]) / den
        o_ref[:, p * LANES:(p + 1) * LANES] = jnp.where(lo_q, o2[0:blk], o2[blk:2 * blk]).astype(BF16)


def _swa(b1, sinks, bias_tab, seq):
    t, nc = b1.shape
    nbs = seq // ATT_BLK
    kern = functools.partial(_swa_kernel, blocks_per_seq=nbs)
    return pl.pallas_call(
        kern,
        grid=(t // ATT_BLK,),
        in_specs=[
            pl.BlockSpec(memory_space=pltpu.SMEM),
            pl.BlockSpec((ATT_BLK, nc), lambda i: (i, 0)),
            pl.BlockSpec((ATT_BLK, nc), lambda i: (jnp.maximum(i - 1, 0), 0)),
            pl.BlockSpec((H_B // 2, 2 * ATT_BLK, 2 * ATT_BLK), lambda i: (0, 0, 0)),
        ],
        out_specs=pl.BlockSpec((ATT_BLK, WB), lambda i: (i, 0)),
        out_shape=jax.ShapeDtypeStruct((t, WB), BF16),
        compiler_params=_cparams(("arbitrary",)),
        name="swa",
    )(sinks, b1, b1, bias_tab)


def _t5_bias_table(rel_bias):
    qi = jnp.arange(ATT_BLK)[:, None]
    ki = jnp.arange(2 * ATT_BLK)[None, :]
    n = jnp.maximum(qi + ATT_BLK - ki, 0)
    max_exact = NUM_BUCKETS // 2
    nf = jnp.maximum(n, 1).astype(F32)
    large = max_exact + (jnp.log(nf / max_exact) / math.log(MAX_DISTANCE / max_exact)
                         * (NUM_BUCKETS - max_exact)).astype(I32)
    bucket = jnp.where(n < max_exact, n, jnp.minimum(large, NUM_BUCKETS - 1))
    onehot = (bucket[..., None] == jnp.arange(NUM_BUCKETS)).astype(F32)
    bias = jnp.einsum('qkb,bh->hqk', onehot, rel_bias.astype(F32), precision=lax.Precision.HIGHEST)
    return bias.reshape(H_B // 2, 2 * ATT_BLK, 2 * ATT_BLK)


def _route_tail(x1, mod_ref, nw2_ref, rwhl_ref, rwhi_ref, rb_ref, tril_ref, carry_ref,
                h2_ref, meta_ref, metat_ref, cnt_ref):
    tm = x1.shape[0]
    h2 = _norm_mod(x1, nw2_ref[...], mod_ref[0, 4:5, :], mod_ref[0, 3:4, :])
    _store_row_tiles(h2_ref, h2)
    h_hi = h2.astype(BF16)
    h_lo = (h2 - h_hi.astype(F32)).astype(BF16)
    r = jnp.dot(h_hi, rwhl_ref[...], preferred_element_type=F32)
    logits = (r[:, 0:LANES] + r[:, LANES:2 * LANES]
              + jnp.dot(h_lo, rwhi_ref[...], preferred_element_type=F32) + rb_ref[...])
    lane = lax.broadcasted_iota(I32, (tm, LANES), 1)
    lane_f = lane.astype(F32)
    big = 1e9
    neg = -jnp.inf
    is_grp = (lane >= R_GRP0) & (lane < R_GRP0 + N_GROUPS)
    lg1 = jnp.where(is_grp, logits, neg)
    m1 = jnp.max(lg1, axis=-1, keepdims=True)
    grp = jnp.min(jnp.where(lg1 == m1, lane_f, big), axis=-1, keepdims=True) - R_GRP0
    p_top = 1.0 / jnp.sum(jnp.where(is_grp, jnp.exp(lg1 - m1), 0.0), axis=-1, keepdims=True)
    lane_grp = jnp.right_shift(lane, EXPERTS_PER_GROUP.bit_length() - 1).astype(F32)
    in_grp = (lane < N_EXPERTS) & (lane_grp == grp)
    l2 = jnp.where(in_grp, logits, neg)
    t1 = jnp.max(l2, axis=-1, keepdims=True)
    e1 = jnp.min(jnp.where(l2 == t1, lane_f, big), axis=-1, keepdims=True)
    l2b = jnp.where(lane_f == e1, neg, l2)
    t2 = jnp.max(l2b, axis=-1, keepdims=True)
    e2 = jnp.min(jnp.where(l2b == t2, lane_f, big), axis=-1, keepdims=True)
    ex = jnp.exp(t2 - t1)
    gate1 = p_top / (1.0 + ex)
    gate2 = p_top * ex / (1.0 + ex)
    hit1 = lane_f == e1
    hit2 = lane_f == e2
    onehot = jnp.where(hit1 | hit2, 1.0, 0.0)
    incl = jnp.dot(tril_ref[...], onehot.astype(BF16), preferred_element_type=F32)
    excl = incl - onehot + carry_ref[...]
    rank1 = jnp.sum(jnp.where(hit1, excl, 0.0), axis=-1, keepdims=True)
    rank2 = jnp.sum(jnp.where(hit2, excl, 0.0), axis=-1, keepdims=True)
    carry_ref[...] = carry_ref[...] + incl[tm - 1:tm, :]
    cnt_ref[...] = carry_ref[...]
    meta = jnp.zeros((tm, LANES), F32)
    for idx, val in ((M_E1, e1), (M_E2, e2), (M_R1, rank1), (M_R2, rank2), (M_G1, gate1), (M_G2, gate2)):
        meta = jnp.where(lane == idx, val, meta)
    meta_ref[...] = meta
    metat_ref[0] = meta.T[0:8, :]


def _chunk_rows(cidx, n):
    return pl.ds(cidx, n, stride=ROW_TILE)


def _store_row_tiles(ref, x):
    for cidx in range(ROW_TILE):
        ref[_chunk_rows(cidx, x.shape[0]), :] = x[:, cidx * LANES:(cidx + 1) * LANES]


def _load_row_tiles(ref):
    n = ref.shape[0] // ROW_TILE
    return jnp.concatenate([ref[_chunk_rows(cidx, n), :] for cidx in range(ROW_TILE)], axis=1)


def _tail_in_specs(d, tm, tps):
    return [
        pl.BlockSpec((1, 6, d), lambda i: (i // tps, 0, 0)),
        pl.BlockSpec((1, d), lambda i: (0, 0)),
        pl.BlockSpec((d, 2 * LANES), lambda i: (0, 0)),
        pl.BlockSpec((d, LANES), lambda i: (0, 0)),
        pl.BlockSpec((1, LANES), lambda i: (0, 0)),
        pl.BlockSpec((tm, tm), lambda i: (0, 0)),
    ]


def _tail_out(t, d, tm):
    specs = [
        pl.BlockSpec((tm, d), lambda i: (i, 0)),
        pl.BlockSpec((tm * ROW_TILE, LANES), lambda i: (i, 0)),
        pl.BlockSpec((tm, LANES), lambda i: (i, 0)),
        pl.BlockSpec((1, 8, tm), lambda i: (i, 0, 0)),
        pl.BlockSpec((1, LANES), lambda i: (0, 0)),
    ]
    shapes = [
        jax.ShapeDtypeStruct((t, d), F32),
        jax.ShapeDtypeStruct((t * ROW_TILE, LANES), F32),
        jax.ShapeDtypeStruct((t, LANES), F32),
        jax.ShapeDtypeStruct((t // tm, 8, tm), F32),
        jax.ShapeDtypeStruct((1, LANES), F32),
    ]
    return specs, shapes


def _outproj_kernel(oa_ref, ob_ref, x_ref, wo_ref, mod_ref, nw2_ref, rwhl_ref, rwhi_ref, rb_ref,
                    tril_ref, x1_ref, h2_ref, meta_ref, metat_ref, cnt_ref, carry_ref):
    @pl.when(pl.program_id(0) == 0)
    def _():
        carry_ref[...] = jnp.zeros(carry_ref.shape, F32)

    y = (jnp.dot(oa_ref[...], wo_ref[0:WA, :], preferred_element_type=F32)
         + jnp.dot(ob_ref[...], wo_ref[WA:WA + WB, :], preferred_element_type=F32))
    x1 = x_ref[...] + mod_ref[0, 2:3, :] * y
    x1_ref[...] = x1
    _route_tail(x1, mod_ref, nw2_ref, rwhl_ref, rwhi_ref, rb_ref, tril_ref, carry_ref,
                h2_ref, meta_ref, metat_ref, cnt_ref)


def _outproj_route(oa, ob, x, w_out_b, mod, nw2, rwhl, rwhi, rb, tril, seq, tm):
    t, d = x.shape
    tps = seq // tm
    out_specs, out_shapes = _tail_out(t, d, tm)
    return pl.pallas_call(
        _outproj_kernel,
        grid=(t // tm,),
        in_specs=[
            pl.BlockSpec((tm, WA), lambda i: (i, 0)),
            pl.BlockSpec((tm, WB), lambda i: (i, 0)),
            pl.BlockSpec((tm, d), lambda i: (i, 0)),
            pl.BlockSpec((WA + WB, d), lambda i: (0, 0)),
        ] + _tail_in_specs(d, tm, tps),
        out_specs=out_specs,
        out_shape=out_shapes,
        scratch_shapes=[pltpu.VMEM((1, LANES), F32)],
        compiler_params=_cparams(("arbitrary",)),
        name="outproj_route",
    )(oa, ob, x, w_out_b, mod, nw2, rwhl, rwhi, rb, tril)


def _pool_kernel(x_ref, nw1_ref, pw_ref, ps_ref, mod_ref, nw2_ref, rwhl_ref, rwhi_ref, rb_ref,
                 tril_ref, x1_ref, h2_ref, meta_ref, metat_ref, cnt_ref, carry_ref, e1_ref, e2_ref, e3_ref,
                 *, blocks_per_seq):
    tp, d = x_ref.shape
    hist = POOL_HIST
    i = pl.program_id(0)

    @pl.when(i == 0)
    def _():
        carry_ref[...] = jnp.zeros(carry_ref.shape, F32)

    @pl.when(i % blocks_per_seq == 0)
    def _():
        e1_ref[0:hist, :] = jnp.zeros((hist, d), F32)

    x = x_ref[...]
    h = _norm_mod(x, nw1_ref[...], mod_ref[0, 1:2, :], mod_ref[0, 0:1, :])
    e1_ref[hist:hist + tp, :] = h
    pos = (i % blocks_per_seq) * tp + lax.broadcasted_iota(I32, (tp, 1), 0)
    gdim = POOL_GROUP_DIM
    ys = []
    for gi, win in enumerate(POOL_WINDOWS):
        c0 = gi * gdim
        levels = win.bit_length() - 1
        src, spare = e1_ref, (e2_ref, e3_ref)
        for m in range(1, levels + 1):
            shift = 1 << (m - 1)
            start = hist - 8 * (levels - m)
            n = hist + tp - start
            cur = src[start:start + n, c0:c0 + gdim] + src[start - shift:start - shift + n, c0:c0 + gdim]
            if m < levels:
                dst = spare[m % 2]
                dst[start:start + n, c0:c0 + gdim] = cur
                src = dst
        cnt = jnp.minimum(pos + 1, win).astype(F32)
        pooled = cur / cnt - h[:, c0:c0 + gdim]
        ys.append(_mm(pooled, pw_ref[gi]))
    e1_ref[0:hist, :] = h[tp - hist:tp, :]
    y = jnp.concatenate(ys, axis=1) * ps_ref[...]
    x1 = x + mod_ref[0, 2:3, :] * y
    x1_ref[...] = x1
    _route_tail(x1, mod_ref, nw2_ref, rwhl_ref, rwhi_ref, rb_ref, tril_ref, carry_ref,
                h2_ref, meta_ref, metat_ref, cnt_ref)


def _pool_route(x, nw1, pool_w_b, pool_scale, mod, nw2, rwhl, rwhi, rb, tril, seq, tp):
    t, d = x.shape
    tps = seq // tp
    out_specs, out_shapes = _tail_out(t, d, tp)
    kern = functools.partial(_pool_kernel, blocks_per_seq=tps)
    return pl.pallas_call(
        kern,
        grid=(t // tp,),
        in_specs=[
            pl.BlockSpec((tp, d), lambda i: (i, 0)),
            pl.BlockSpec((1, d), lambda i: (0, 0)),
            pl.BlockSpec((4, POOL_GROUP_DIM, POOL_GROUP_DIM), lambda i: (0, 0, 0)),
            pl.BlockSpec((1, d), lambda i: (0, 0)),
        ] + _tail_in_specs(d, tp, tps),
        out_specs=out_specs,
        out_shape=out_shapes,
        scratch_shapes=[
            pltpu.VMEM((1, LANES), F32),
            pltpu.VMEM((tp + POOL_HIST, d), F32),
            pltpu.VMEM((tp + POOL_HIST, d), F32),
            pltpu.VMEM((tp + POOL_HIST, d), F32),
        ],
        compiler_params=_cparams(("arbitrary",)),
        name="pool_route",
    )(x, nw1, pool_w_b, pool_scale, mod, nw2, rwhl, rwhi, rb, tril)


def _dispatch_kernel(dest_ref, fill_ref, h_ref, xs_ref, zero_ref, sem):
    td = h_ref.shape[0] // ROW_TILE

    def row_tile(r, n=1):
        return pl.ds(pl.multiple_of(r * ROW_TILE, ROW_TILE), n * ROW_TILE)

    @pl.when(pl.program_id(0) == 0)
    def _():
        zero_ref[...] = jnp.zeros(zero_ref.shape, zero_ref.dtype)

        def fill(e, carry):
            pos = fill_ref[0, e]
            n = fill_ref[1, e]
            sizes = [1 << bit for bit in reversed(range(MOE_ROWS.bit_length() - 1))]
            for size in sizes:
                take = (n & size) != 0

                @pl.when(take)
                def _():
                    pltpu.make_async_copy(zero_ref.at[row_tile(0, size)], xs_ref.at[row_tile(pos, size)], sem).start()

                pos = pos + jnp.where(take, size, 0)
            for size in sizes:
                @pl.when((n & size) != 0)
                def _():
                    pltpu.make_async_copy(zero_ref.at[row_tile(0, size)], xs_ref.at[row_tile(0, size)], sem).wait()

            return carry

        lax.fori_loop(0, N_EXPERTS, fill, 0)

    def issue(tk, carry):
        for s in range(2):
            d = dest_ref[0, 0, s * td + tk]
            pltpu.make_async_copy(h_ref.at[row_tile(tk)], xs_ref.at[row_tile(d)], sem).start(priority=s)
        return carry

    lax.fori_loop(0, td, issue, 0)

    for s in range(2):
        pltpu.make_async_copy(h_ref, xs_ref.at[row_tile(0, td)], sem).wait()


def _dispatch(dest3, fill, h2, n_rows, td):
    t = h2.shape[0] // ROW_TILE
    return pl.pallas_call(
        _dispatch_kernel,
        grid=(t // td,),
        in_specs=[
            pl.BlockSpec((1, 1, 2 * td), lambda i: (i, 0, 0), memory_space=pltpu.SMEM),
            pl.BlockSpec(memory_space=pltpu.SMEM),
            pl.BlockSpec((td * ROW_TILE, LANES), lambda i: (i, 0)),
        ],
        out_specs=pl.BlockSpec(memory_space=pl.ANY),
        out_shape=jax.ShapeDtypeStruct((n_rows * ROW_TILE, LANES), h2.dtype),
        scratch_shapes=[pltpu.VMEM((MOE_ROWS // 2 * ROW_TILE, LANES), h2.dtype), pltpu.SemaphoreType.DMA(())],
        compiler_params=_cparams(("arbitrary",)),
        name="moe_dispatch",
    )(dest3, fill, h2)


def _expert_kernel(be_ref, nu_ref, xs_ref, wg_ref, wu_ref, wd_ref, y_ref, wgb_ref, wub_ref, wdb_ref):
    i = pl.program_id(0)
    prev = be_ref[jnp.maximum(i - 1, 0)]
    changed = (i == 0) | (be_ref[i] != prev)

    @pl.when(changed)
    def _():
        wgb_ref[...] = wg_ref[0, 0].astype(BF16)
        wub_ref[...] = wu_ref[0, 0].astype(BF16)
        wdb_ref[...] = wd_ref[0, 0].astype(BF16)

    @pl.when(i < nu_ref[0])
    def _():
        xb = _load_row_tiles(xs_ref).astype(BF16)
        g = jnp.dot(xb, wgb_ref[...], preferred_element_type=F32)
        u = jnp.dot(xb, wub_ref[...], preferred_element_type=F32)
        a = (_silu(g) * u).astype(BF16)
        _store_row_tiles(y_ref, jnp.dot(a, wdb_ref[...], preferred_element_type=F32))

    @pl.when(i >= nu_ref[0])
    def _():
        y_ref[...] = jnp.zeros(y_ref.shape, F32)


def _experts(blk_expert, n_used, xs, w_gate, w_up, w_down, layer):
    n_rows = xs.shape[0] // ROW_TILE
    d = ROW_TILE * LANES
    blk = (MOE_ROWS * ROW_TILE, LANES)
    n_blk = n_rows // MOE_ROWS
    return pl.pallas_call(
        _expert_kernel,
        grid_spec=pltpu.PrefetchScalarGridSpec(
            num_scalar_prefetch=2,
            grid=(n_blk,),
            in_specs=[
                pl.BlockSpec(blk, lambda i, be, nu: (i, 0)),
                pl.BlockSpec((1, 1, d, D_FF), lambda i, be, nu: (layer, be[i], 0, 0)),
                pl.BlockSpec((1, 1, d, D_FF), lambda i, be, nu: (layer, be[i], 0, 0)),
                pl.BlockSpec((1, 1, D_FF, d), lambda i, be, nu: (layer, be[i], 0, 0)),
            ],
            out_specs=pl.BlockSpec(blk, lambda i, be, nu: (i, 0)),
            scratch_shapes=[
                pltpu.VMEM((d, D_FF), BF16),
                pltpu.VMEM((d, D_FF), BF16),
                pltpu.VMEM((D_FF, d), BF16),
            ],
        ),
        out_shape=jax.ShapeDtypeStruct(xs.shape, F32),
        compiler_params=_cparams(("arbitrary",)),
        name="moe_experts",
    )(blk_expert, n_used, xs, w_gate, w_up, w_down)


def _combine_kernel(dest_ref, ys_ref, x1_ref, meta_ref, mod_ref, fw_ref, o_ref, buf_ref, sem, *, final_norm):
    tc = x1_ref.shape[0]

    def row_tile(r):
        return pl.ds(pl.multiple_of(r * ROW_TILE, ROW_TILE), ROW_TILE)

    def issue(tk, carry):
        for s in range(2):
            d = dest_ref[0, 0, s * tc + tk]
            pltpu.make_async_copy(ys_ref.at[row_tile(d)], buf_ref.at[s, row_tile(tk)], sem).start(priority=s)
        return carry

    lax.fori_loop(0, tc, issue, 0)

    for s in range(2):
        pltpu.make_async_copy(ys_ref.at[pl.ds(0, tc * ROW_TILE)], buf_ref.at[s], sem).wait()

    meta = meta_ref[...]
    g1 = meta[:, M_G1:M_G1 + 1]
    g2 = meta[:, M_G2:M_G2 + 1]
    n_chunks = ROW_TILE
    ssq = jnp.zeros((tc, 1), F32)
    for cidx in range(n_chunks):
        sl = slice(cidx * LANES, (cidx + 1) * LANES)
        rows = _chunk_rows(cidx, tc)
        moe = buf_ref[0, rows, :] * g1 + buf_ref[1, rows, :] * g2
        x2 = x1_ref[:, sl] + mod_ref[0, 5:6, sl] * moe
        o_ref[:, sl] = x2
        if final_norm:
            ssq = ssq + jnp.sum(x2 * x2, axis=-1, keepdims=True)
    if final_norm:
        inv = lax.rsqrt(ssq * (1.0 / (n_chunks * LANES)) + EPS)
        for cidx in range(n_chunks):
            sl = slice(cidx * LANES, (cidx + 1) * LANES)
            o_ref[:, sl] = o_ref[:, sl] * inv * fw_ref[:, sl]


def _combine(dest3, ys, x1, meta, mod, fw, seq, tc, final_norm):
    t, d = x1.shape
    tps = seq // tc
    kern = functools.partial(_combine_kernel, final_norm=final_norm)
    return pl.pallas_call(
        kern,
        grid=(t // tc,),
        in_specs=[
            pl.BlockSpec((1, 1, 2 * tc), lambda i: (i, 0, 0), memory_space=pltpu.SMEM),
            pl.BlockSpec(memory_space=pl.ANY),
            pl.BlockSpec((tc, d), lambda i: (i, 0)),
            pl.BlockSpec((tc, LANES), lambda i: (i, 0)),
            pl.BlockSpec((1, 6, d), lambda i: (i // tps, 0, 0)),
            pl.BlockSpec((1, d), lambda i: (0, 0)),
        ],
        out_specs=pl.BlockSpec((tc, d), lambda i: (i, 0)),
        out_shape=jax.ShapeDtypeStruct((t, d), F32),
        scratch_shapes=[pltpu.VMEM((2, tc * ROW_TILE, LANES), F32), pltpu.SemaphoreType.DMA(())],
        compiler_params=_cparams(("arbitrary",)),
        name="moe_combine",
    )(dest3, ys, x1, meta, mod, fw)


def _moe(x1, h2, meta, metat, cnt, mod, fw, w_gate, w_up, w_down, layer, seq, tile, final_norm):
    t, d = x1.shape
    n_rows = -(-(2 * t + N_EXPERTS * (MOE_ROWS - 1)) // MOE_ROWS) * MOE_ROWS
    n_blk = n_rows // MOE_ROWS
    counts = cnt[0, :N_EXPERTS].astype(I32)
    padded = (counts + MOE_ROWS - 1) // MOE_ROWS * MOE_ROWS
    pad_end = jnp.cumsum(padded)
    pad_start = pad_end - padded
    expert = metat[:, M_E1:M_E2 + 1, :].astype(I32)
    rank = metat[:, M_R1:M_R2 + 1, :].astype(I32)
    start_of = jnp.sum(jnp.where(expert[..., None] == jnp.arange(N_EXPERTS, dtype=I32), pad_start, 0), axis=-1)
    dest3 = (start_of + rank).reshape(t // tile, 1, 2 * tile)
    blk_row0 = jnp.arange(n_blk, dtype=I32) * MOE_ROWS
    blk_expert = jnp.minimum(jnp.sum((pad_end[None, :] <= blk_row0[:, None]).astype(I32), axis=1),
                             N_EXPERTS - 1)
    n_used = (pad_end[-1:] // MOE_ROWS).astype(I32)
    fill = jnp.stack([pad_start + counts, padded - counts])
    xs = _dispatch(dest3, fill, h2, n_rows, tile)
    ys = _experts(blk_expert, n_used, xs, w_gate, w_up, w_down, layer)
    return _combine(dest3, ys, x1, meta, mod, fw, seq, tile, final_norm)


def _router_params(r1_w, r1_b, r2_w, r2_b):
    d = r1_w.shape[0]
    rw = jnp.zeros((d, LANES), F32).at[:, :N_EXPERTS].set(r2_w).at[:, R_GRP0:R_GRP0 + N_GROUPS].set(r1_w)
    rb = jnp.zeros((1, LANES), F32).at[0, :N_EXPERTS].set(r2_b).at[0, R_GRP0:R_GRP0 + N_GROUPS].set(r1_b)
    hi = rw.astype(BF16)
    lo = (rw - hi.astype(F32)).astype(BF16)
    return jnp.concatenate([hi, lo], axis=1), hi, rb


def kernel(x, c, ada_w, ada_b, norm_mix_w, norm_ffn_w, ab_w_in, ab_conv_w, ab_a_log, ab_dt_bias, ab_onorm_w, ab_sinks, rel_bias, ab_w_out, pool_w, pool_scale, r1_w, r1_b, r2_w, r2_b, moe_w_gate, moe_w_up, moe_w_down, final_norm_w):
    bsz, seq, d = x.shape
    t = bsz * seq
    tile = min(512, seq)
    lb = min(256, seq)
    xf = x.reshape(t, d)

    mod_all = _ada(c, ada_w, ada_b).reshape(DEPTH, bsz, 6, d)
    tril = jnp.tril(jnp.ones((tile, tile), BF16))
    fw = final_norm_w.reshape(1, d)

    w_in = ab_w_in[0]
    cuts = (0, WA, 2 * WA, 3 * WA, 4 * WA, 4 * WA + H_A, 4 * WA + 2 * H_A)
    qkvz = w_in[:, :cuts[4]]
    ba = w_in[:, cuts[4]:cuts[6]]
    qkv_b = w_in[:, cuts[6]:]
    w_all = jnp.concatenate([qkvz, qkv_b, ba, jnp.zeros((d, LANES - 2 * H_A), F32)], axis=1).astype(BF16)
    a1, b1, g1 = _inproj(xf, mod_all[0], norm_mix_w[0:1], w_all, seq, tile)

    pad = jnp.zeros((LANES - 2 * H_A,), F32)
    alog_row = jnp.concatenate([jnp.zeros((H_A,), F32), ab_a_log[0], pad]).reshape(1, LANES)
    dt_row = jnp.concatenate([jnp.zeros((H_A,), F32), ab_dt_bias[0], pad]).reshape(1, LANES)
    onw_row = jnp.concatenate([ab_onorm_w[0], ab_onorm_w[0]]).reshape(1, LANES)
    oa = _deltanet(a1, g1, ab_conv_w[0], alog_row, dt_row, onw_row, seq, lb)
    ob = _swa(b1, ab_sinks[0], _t5_bias_table(rel_bias), seq)

    rwhl, rwhi, rb = _router_params(r1_w[0], r1_b[0], r2_w[0], r2_b[0])
    x1, h2, meta, metat, cnt = _outproj_route(oa, ob, xf, ab_w_out[0].astype(BF16), mod_all[0], norm_ffn_w[0:1],
                                       rwhl, rwhi, rb, tril, seq, tile)
    x2 = _moe(x1, h2, meta, metat, cnt, mod_all[0], fw, moe_w_gate, moe_w_up, moe_w_down, 0,
              seq, tile, False)

    rwhl, rwhi, rb = _router_params(r1_w[1], r1_b[1], r2_w[1], r2_b[1])
    x3, h2, meta, metat, cnt = _pool_route(x2, norm_mix_w[1:2], pool_w[0].astype(BF16), pool_scale[0].reshape(1, d),
                                    mod_all[1], norm_ffn_w[1:2], rwhl, rwhi, rb, tril, seq, tile)
    out = _moe(x3, h2, meta, metat, cnt, mod_all[1], fw, moe_w_gate, moe_w_up, moe_w_down, 1,
               seq, tile, True)
    return out.reshape(bsz, seq, d)
```

```python
import functools
import math

import jax
import jax.numpy as jnp
from jax import lax
from jax.experimental import pallas as pl
from jax.experimental.pallas import tpu as pltpu

F32 = jnp.float32
BF16 = jnp.bfloat16
I32 = jnp.int32

D_MODEL = 1024
DEPTH = 2
H_A = 8
DK_A = 64
CONV_K = 4
CHUNK = 64
H_B = 8
H_B_KV = 2
DH_B = 64
WINDOW = 128
ATT_BLK = 128
NUM_BUCKETS = 32
MAX_DISTANCE = 128
POOL_WINDOWS = (2, 4, 8, 16)
POOL_GROUP_DIM = D_MODEL // 4
POOL_HIST = 32
N_GROUPS = 4
EXPERTS_PER_GROUP = 8
N_EXPERTS = N_GROUPS * EXPERTS_PER_GROUP
D_FF = D_MODEL // 2
EPS = 1e-6

WA = H_A * DK_A
WB = H_B * DH_B
WKV = H_B_KV * DH_B
N_PAIR = H_A // 2
LANES = 128
ROW_TILE = D_MODEL // LANES
MOE_ROWS = 512
VMEM_LIMIT = 56 * 1024 * 1024

M_E1, M_E2, M_R1, M_R2, M_G1, M_G2 = 0, 1, 2, 3, 4, 5
R_GRP0 = N_EXPERTS


def _cparams(sem):
    return pltpu.CompilerParams(dimension_semantics=sem, vmem_limit_bytes=VMEM_LIMIT)


def _mm(a, b):
    return jnp.dot(a.astype(BF16), b.astype(BF16), preferred_element_type=F32)


def _mm_nt(a, b):
    return lax.dot_general(a.astype(BF16), b.astype(BF16), (((1,), (1,)), ((), ())),
                           preferred_element_type=F32)


def _mm_x3(a, b):
    ah = a.astype(BF16)
    al = (a - ah.astype(F32)).astype(BF16)
    bh = b.astype(BF16)
    bl = (b - bh.astype(F32)).astype(BF16)
    return jnp.dot(jnp.concatenate([ah, al, ah], axis=1), jnp.concatenate([bh, bh, bl], axis=0),
                   preferred_element_type=F32)


def _split3(x):
    hi = x.astype(BF16)
    r = x - hi.astype(F32)
    mid = r.astype(BF16)
    lo = (r - mid.astype(F32)).astype(BF16)
    return hi, mid, lo


def _norm_mod(x, nw, sc, sh):
    ms = jnp.mean(x * x, axis=-1, keepdims=True)
    return (x * lax.rsqrt(ms + EPS) * nw) * (1.0 + sc) + sh


def _silu(x):
    return x * jax.nn.sigmoid(x)


def _ada_kernel(ct_ref, w_ref, b_ref, o_ref):
    nb = ct_ref.shape[1]
    d = ct_ref.shape[0]
    rows = []
    for b in range(nb):
        acc = None
        for k0 in range(0, d, LANES):
            ct = ct_ref[k0:k0 + LANES, b:b + 1]
            part = jnp.sum(_silu(ct) * w_ref[0, k0:k0 + LANES, :], axis=0, keepdims=True)
            acc = part if acc is None else acc + part
        rows.append(acc)
    o_ref[0] = jnp.concatenate(rows, axis=0) + b_ref[0]


def _ada(c, ada_w, ada_b):
    nb, d = c.shape
    depth, _, n6 = ada_w.shape
    cols = 1536
    return pl.pallas_call(
        _ada_kernel,
        grid=(depth, n6 // cols),
        in_specs=[
            pl.BlockSpec((d, nb), lambda l, j: (0, 0)),
            pl.BlockSpec((1, d, cols), lambda l, j: (l, 0, j)),
            pl.BlockSpec((1, 1, cols), lambda l, j: (l, 0, j)),
        ],
        out_specs=pl.BlockSpec((1, nb, cols), lambda l, j: (l, 0, j)),
        out_shape=jax.ShapeDtypeStruct((depth, nb, n6), F32),
        compiler_params=_cparams(("arbitrary", "arbitrary")),
        name="ada_mod",
    )(c.T, ada_w, ada_b.reshape(depth, 1, n6))


def _inproj_kernel(x_ref, mod_ref, nw_ref, w_ref, a_ref, b_ref, g_ref):
    h = _norm_mod(x_ref[...], nw_ref[...], mod_ref[0, 1:2, :], mod_ref[0, 0:1, :])
    hb = h.astype(BF16)
    na = a_ref.shape[1]
    nbb = b_ref.shape[1]
    a_ref[...] = jnp.dot(hb, w_ref[:, 0:na], preferred_element_type=F32)
    b_ref[...] = jnp.dot(hb, w_ref[:, na:na + nbb], preferred_element_type=F32).astype(BF16)
    g_ref[...] = jnp.dot(hb, w_ref[:, na + nbb:], preferred_element_type=F32)


def _inproj(x, mod, nw, w_all, seq, tm):
    t, d = x.shape
    tps = seq // tm
    na, nbb, ng = 4 * WA, WB + 2 * WKV, LANES
    return pl.pallas_call(
        _inproj_kernel,
        grid=(t // tm,),
        in_specs=[
            pl.BlockSpec((tm, d), lambda i: (i, 0)),
            pl.BlockSpec((1, 6, d), lambda i: (i // tps, 0, 0)),
            pl.BlockSpec((1, d), lambda i: (0, 0)),
            pl.BlockSpec((d, na + nbb + ng), lambda i: (0, 0)),
        ],
        out_specs=[
            pl.BlockSpec((tm, na), lambda i: (i, 0)),
            pl.BlockSpec((tm, nbb), lambda i: (i, 0)),
            pl.BlockSpec((tm, ng), lambda i: (i, 0)),
        ],
        out_shape=[
            jax.ShapeDtypeStruct((t, na), F32),
            jax.ShapeDtypeStruct((t, nbb), BF16),
            jax.ShapeDtypeStruct((t, ng), F32),
        ],
        compiler_params=_cparams(("arbitrary",)),
        name="inproj",
    )(x, mod, nw, w_all)


def _lane_half(shape):
    return lax.broadcasted_iota(I32, shape, 1) < (LANES // 2)


def _half_sums(x2, lo):
    se = jnp.sum(jnp.where(lo, x2, 0.0), axis=-1, keepdims=True)
    so = jnp.sum(jnp.where(lo, 0.0, x2), axis=-1, keepdims=True)
    return jnp.where(lo, se, so)


def _deltanet_kernel(a_ref, gp_ref, cw_ref, alog_ref, dt_ref, onw_ref, o_ref,
                     ext_ref, s_ref, q_ref, k_ref, v_ref, gf_ref, bf_ref, *, blocks_per_seq):
    lb = a_ref.shape[0]
    i = pl.program_id(0)

    @pl.when(i % blocks_per_seq == 0)
    def _():
        ext_ref[0:8, :] = jnp.zeros((8, ext_ref.shape[1]), F32)
        s_ref[...] = jnp.zeros(s_ref.shape, F32)

    lo_b = _lane_half((lb, LANES))
    for part, dst in ((0, q_ref), (1, k_ref), (2, v_ref)):
        for p in range(N_PAIR):
            c0 = part * WA + p * LANES
            xin = a_ref[:, c0:c0 + LANES]
            ext_ref[8:8 + lb, c0:c0 + LANES] = xin
            acc = cw_ref[CONV_K - 1:CONV_K, c0:c0 + LANES] * xin
            for j in range(CONV_K - 1):
                acc = acc + cw_ref[j:j + 1, c0:c0 + LANES] * ext_ref[pl.ds(8 - (CONV_K - 1) + j, lb), c0:c0 + LANES]
            ext_ref[0:8, c0:c0 + LANES] = xin[lb - 8:lb, :]
            y = _silu(acc)
            if part < 2:
                y = y * lax.rsqrt(_half_sums(y * y, lo_b) + EPS)
                if part == 0:
                    y = y * (DK_A ** -0.5)
            dst[:, p * LANES:(p + 1) * LANES] = y

    gp = gp_ref[...]
    bf_ref[...] = jax.nn.sigmoid(gp)
    xg = gp + dt_ref[...]
    softplus = jnp.maximum(xg, 0.0) + jnp.log(1.0 + jnp.exp(-jnp.abs(xg)))
    gf_ref[...] = -jnp.exp(alog_ref[...]) * softplus

    c = CHUNK
    c2 = 2 * c
    row = lax.broadcasted_iota(I32, (c2, c2), 0)
    col = lax.broadcasted_iota(I32, (c2, c2), 1)
    same_blk = (row < c) == (col < c)
    incl = same_blk & (row >= col)
    strict = same_blk & (row > col)
    eye = row == col
    eye_f = jnp.where(eye, 1.0, 0.0)
    lo = _lane_half((c, LANES))

    def stack(x):
        return jnp.where(same_blk, jnp.concatenate([x, x], axis=0), 0.0)

    def unstack(x):
        return jnp.where(lo, x[0:c], x[c:c2])

    rb = lax.broadcasted_iota(I32, (lb, lb), 0)
    cb = lax.broadcasted_iota(I32, (lb, lb), 1)
    shift = CHUNK.bit_length() - 1
    tril_chunks = jnp.where((jnp.right_shift(rb, shift) == jnp.right_shift(cb, shift)) & (rb >= cb),
                            1.0, 0.0).astype(BF16)
    g_hi, g_mid, g_lo = _split3(gf_ref[...])
    gf_ref[...] = (jnp.dot(tril_chunks, g_hi, preferred_element_type=F32)
                   + jnp.dot(tril_chunks, g_mid, preferred_element_type=F32)
                   + jnp.dot(tril_chunks, g_lo, preferred_element_type=F32))

    pairs = range(N_PAIR)

    def chunk_body(ci, carry):
        r0 = pl.multiple_of(ci * c, c)
        gcc = gf_ref[pl.ds(r0, c), :]
        bfc = bf_ref[pl.ds(r0, c), :]
        q = [q_ref[pl.ds(r0, c), p * LANES:(p + 1) * LANES] for p in pairs]
        k = [k_ref[pl.ds(r0, c), p * LANES:(p + 1) * LANES] for p in pairs]
        v = [v_ref[pl.ds(r0, c), p * LANES:(p + 1) * LANES] for p in pairs]

        def lane_bcast(x, j):
            return jnp.broadcast_to(x[:, j:j + 1], (c, LANES))

        beta = [jnp.where(lo, lane_bcast(bfc, 2 * p), lane_bcast(bfc, 2 * p + 1)) for p in pairs]
        gce = [lane_bcast(gcc, H_A + 2 * p) for p in pairs]
        gco = [lane_bcast(gcc, H_A + 2 * p + 1) for p in pairs]
        gcol = [jnp.concatenate([gce[p], gco[p]], axis=0) for p in pairs]
        gc = [jnp.where(lo, gce[p], gco[p]) for p in pairs]
        grow = [jnp.sum(jnp.where(eye, gcol[p], 0.0), axis=0, keepdims=True) for p in pairs]
        decay = [jnp.where(incl, jnp.exp(jnp.where(incl, gcol[p] - grow[p], 0.0)), 0.0) for p in pairs]
        egc = [jnp.exp(gc[p]) for p in pairs]
        kb = [k[p] * beta[p] for p in pairs]
        k_st = [stack(k[p]) for p in pairs]
        n_mat = [jnp.where(strict, _mm_nt(stack(kb[p]), k_st[p]) * decay[p], 0.0) for p in pairs]
        a_qk = [_mm_nt(stack(q[p]), k_st[p]) * decay[p] for p in pairs]
        x_inv = [eye_f - n_mat[p] for p in pairs]
        y_pow = [_mm_x3(n_mat[p], n_mat[p]) for p in pairs]
        for it in range(5):
            x_inv = [x_inv[p] + _mm_x3(x_inv[p], y_pow[p]) for p in pairs]
            if it < 4:
                y_pow = [_mm_x3(y_pow[p], y_pow[p]) for p in pairs]
        sol = []
        for p in pairs:
            vb = v[p] * beta[p]
            kbg = kb[p] * egc[p]
            rhs = jnp.concatenate([jnp.concatenate([vb, vb], axis=0),
                                   jnp.concatenate([kbg, kbg], axis=0)], axis=1)
            sol.append(_mm(x_inv[p], rhs))
        u = [unstack(sol[p][:, 0:LANES]) for p in pairs]
        w = [unstack(sol[p][:, LANES:2 * LANES]) for p in pairs]
        state = [s_ref[p] for p in pairs]
        wq = [_mm(jnp.concatenate([w[p], q[p] * egc[p]], axis=0), state[p]) for p in pairs]
        v_new = [u[p] - wq[p][0:c] for p in pairs]
        av = [_mm(a_qk[p], jnp.concatenate([v_new[p], v_new[p]], axis=0)) for p in pairs]
        for p in pairs:
            k_dec = k[p] * jnp.exp(gc[p][c - 1:c, :] - gc[p])
            g_tot = jnp.where(row < c, jnp.exp(gcol[p][c - 1:c, :]), jnp.exp(gcol[p][c2 - 1:c2, :]))
            s_ref[p] = state[p] * g_tot + jnp.where(same_blk, _mm(k_dec.T, v_new[p]), 0.0)
        for p in pairs:
            out = wq[p][c:c2] + unstack(av[p])
            ms = _half_sums(out * out, lo) * (1.0 / DK_A)
            z = a_ref[pl.ds(r0, c), 3 * WA + p * LANES:3 * WA + (p + 1) * LANES]
            res = out * lax.rsqrt(ms + EPS) * onw_ref[...] * _silu(z)
            o_ref[pl.ds(r0, c), p * LANES:(p + 1) * LANES] = res.astype(BF16)
        return carry

    lax.fori_loop(0, lb // c, chunk_body, 0, unroll=True)


def _deltanet(a1, g1, conv_w, alog_row, dt_row, onw_row, seq, lb):
    t = a1.shape[0]
    kern = functools.partial(_deltanet_kernel, blocks_per_seq=seq // lb)
    return pl.pallas_call(
        kern,
        grid=(t // lb,),
        in_specs=[
            pl.BlockSpec((lb, 4 * WA), lambda i: (i, 0)),
            pl.BlockSpec((lb, LANES), lambda i: (i, 0)),
            pl.BlockSpec((CONV_K, 3 * WA), lambda i: (0, 0)),
            pl.BlockSpec((1, LANES), lambda i: (0, 0)),
            pl.BlockSpec((1, LANES), lambda i: (0, 0)),
            pl.BlockSpec((1, LANES), lambda i: (0, 0)),
        ],
        out_specs=pl.BlockSpec((lb, WA), lambda i: (i, 0)),
        out_shape=jax.ShapeDtypeStruct((t, WA), BF16),
        scratch_shapes=[
            pltpu.VMEM((lb + 8, 3 * WA), F32),
            pltpu.VMEM((N_PAIR, LANES, LANES), F32),
            pltpu.VMEM((lb, WA), F32),
            pltpu.VMEM((lb, WA), F32),
            pltpu.VMEM((lb, WA), F32),
            pltpu.VMEM((lb, LANES), F32),
            pltpu.VMEM((lb, LANES), F32),
        ],
        compiler_params=_cparams(("arbitrary",)),
        name="deltanet",
    )(a1, g1, conv_w, alog_row, dt_row, onw_row)


def _swa_kernel(sink_ref, cur_ref, prev_ref, bias_ref, o_ref, *, blocks_per_seq):
    blk = ATT_BLK
    i = pl.program_id(0)
    first = (i % blocks_per_seq) == 0
    row = lax.broadcasted_iota(I32, (2 * blk, 2 * blk), 0)
    col = lax.broadcasted_iota(I32, (2 * blk, 2 * blk), 1)
    qi = jnp.where(row < blk, row, row - blk)
    dist = qi + blk - col
    first_key = jnp.where(first, blk, 0)
    valid = (dist >= 0) & (dist < WINDOW) & (col >= first_key)
    lo_kv = _lane_half((2 * blk, LANES))
    lo_q = _lane_half((blk, LANES))
    rowq = lax.broadcasted_iota(I32, (2 * blk, LANES), 0)
    q_keep = (rowq < blk) == lo_kv
    top_rows = lax.broadcasted_iota(I32, (2 * blk, 1), 0) < blk

    k0 = WB
    v0 = WB + WKV
    kk = jnp.concatenate([prev_ref[:, k0:k0 + WKV], cur_ref[:, k0:k0 + WKV]], axis=0).astype(F32)
    vv = jnp.concatenate([prev_ref[:, v0:v0 + WKV], cur_ref[:, v0:v0 + WKV]], axis=0).astype(F32)
    kk_sw = pltpu.roll(kk, LANES // 2, axis=1)
    vv_sw = pltpu.roll(vv, LANES // 2, axis=1)
    k_dup = (jnp.where(lo_kv, kk, kk_sw), jnp.where(lo_kv, kk_sw, kk))
    v_dup = (jnp.where(lo_kv, vv, vv_sw), jnp.where(lo_kv, vv_sw, vv))

    for p in range(H_B // 2):
        j = (2 * p) // (H_B // H_B_KV)
        qp = cur_ref[:, p * LANES:(p + 1) * LANES].astype(F32) * (DH_B ** -0.5)
        q_st = jnp.where(q_keep, jnp.concatenate([qp, qp], axis=0), 0.0)
        s = _mm_nt(q_st, k_dup[j])
        s = jnp.where(valid, s + bias_ref[p], -1e30)
        sink = jnp.where(top_rows, sink_ref[2 * p], sink_ref[2 * p + 1])
        m = jnp.maximum(jnp.max(s, axis=-1, keepdims=True), sink)
        pe = jnp.exp(s - m)
        den = jnp.sum(pe, axis=-1, keepdims=True) + jnp.exp(sink - m)
        o2 = _mm(pe, v_dup[j]) / den
        o_ref[:, p * LANES:(p + 1) * LANES] = jnp.where(lo_q, o2[0:blk], o2[blk:2 * blk]).astype(BF16)


def _swa(b1, sinks, bias_tab, seq):
    t, nc = b1.shape
    nbs = seq // ATT_BLK
    kern = functools.partial(_swa_kernel, blocks_per_seq=nbs)
    return pl.pallas_call(
        kern,
        grid=(t // ATT_BLK,),
        in_specs=[
            pl.BlockSpec(memory_space=pltpu.SMEM),
            pl.BlockSpec((ATT_BLK, nc), lambda i: (i, 0)),
            pl.BlockSpec((ATT_BLK, nc), lambda i: (jnp.maximum(i - 1, 0), 0)),
            pl.BlockSpec((H_B // 2, 2 * ATT_BLK, 2 * ATT_BLK), lambda i: (0, 0, 0)),
        ],
        out_specs=pl.BlockSpec((ATT_BLK, WB), lambda i: (i, 0)),
        out_shape=jax.ShapeDtypeStruct((t, WB), BF16),
        compiler_params=_cparams(("arbitrary",)),
        name="swa",
    )(sinks, b1, b1, bias_tab)


def _t5_bias_table(rel_bias):
    qi = jnp.arange(ATT_BLK)[:, None]
    ki = jnp.arange(2 * ATT_BLK)[None, :]
    n = jnp.maximum(qi + ATT_BLK - ki, 0)
    max_exact = NUM_BUCKETS // 2
    nf = jnp.maximum(n, 1).astype(F32)
    large = max_exact + (jnp.log(nf / max_exact) / math.log(MAX_DISTANCE / max_exact)
                         * (NUM_BUCKETS - max_exact)).astype(I32)
    bucket = jnp.where(n < max_exact, n, jnp.minimum(large, NUM_BUCKETS - 1))
    onehot = (bucket[..., None] == jnp.arange(NUM_BUCKETS)).astype(F32)
    bias = jnp.einsum('qkb,bh->hqk', onehot, rel_bias.astype(F32), precision=lax.Precision.HIGHEST)
    return bias.reshape(H_B // 2, 2 * ATT_BLK, 2 * ATT_BLK)


def _route_tail(x1, mod_ref, nw2_ref, rwhl_ref, rwhi_ref, rb_ref, tril_ref, carry_ref,
                h2_ref, meta_ref, metat_ref, cnt_ref):
    tm = x1.shape[0]
    h2 = _norm_mod(x1, nw2_ref[...], mod_ref[0, 4:5, :], mod_ref[0, 3:4, :])
    _store_row_tiles(h2_ref, h2)
    h_hi = h2.astype(BF16)
    h_lo = (h2 - h_hi.astype(F32)).astype(BF16)
    r = jnp.dot(h_hi, rwhl_ref[...], preferred_element_type=F32)
    logits = (r[:, 0:LANES] + r[:, LANES:2 * LANES]
              + jnp.dot(h_lo, rwhi_ref[...], preferred_element_type=F32) + rb_ref[...])
    lane = lax.broadcasted_iota(I32, (tm, LANES), 1)
    lane_f = lane.astype(F32)
    big = 1e9
    neg = -jnp.inf
    is_grp = (lane >= R_GRP0) & (lane < R_GRP0 + N_GROUPS)
    lg1 = jnp.where(is_grp, logits, neg)
    m1 = jnp.max(lg1, axis=-1, keepdims=True)
    grp = jnp.min(jnp.where(lg1 == m1, lane_f, big), axis=-1, keepdims=True) - R_GRP0
    p_top = 1.0 / jnp.sum(jnp.where(is_grp, jnp.exp(lg1 - m1), 0.0), axis=-1, keepdims=True)
    lane_grp = jnp.right_shift(lane, EXPERTS_PER_GROUP.bit_length() - 1).astype(F32)
    in_grp = (lane < N_EXPERTS) & (lane_grp == grp)
    l2 = jnp.where(in_grp, logits, neg)
    t1 = jnp.max(l2, axis=-1, keepdims=True)
    e1 = jnp.min(jnp.where(l2 == t1, lane_f, big), axis=-1, keepdims=True)
    l2b = jnp.where(lane_f == e1, neg, l2)
    t2 = jnp.max(l2b, axis=-1, keepdims=True)
    e2 = jnp.min(jnp.where(l2b == t2, lane_f, big), axis=-1, keepdims=True)
    ex = jnp.exp(t2 - t1)
    gate1 = p_top / (1.0 + ex)
    gate2 = p_top * ex / (1.0 + ex)
    hit1 = lane_f == e1
    hit2 = lane_f == e2
    onehot = jnp.where(hit1 | hit2, 1.0, 0.0)
    incl = jnp.dot(tril_ref[...], onehot.astype(BF16), preferred_element_type=F32)
    excl = incl - onehot + carry_ref[...]
    rank1 = jnp.sum(jnp.where(hit1, excl, 0.0), axis=-1, keepdims=True)
    rank2 = jnp.sum(jnp.where(hit2, excl, 0.0), axis=-1, keepdims=True)
    carry_ref[...] = carry_ref[...] + incl[tm - 1:tm, :]
    cnt_ref[...] = carry_ref[...]
    meta = jnp.zeros((tm, LANES), F32)
    for idx, val in ((M_E1, e1), (M_E2, e2), (M_R1, rank1), (M_R2, rank2), (M_G1, gate1), (M_G2, gate2)):
        meta = jnp.where(lane == idx, val, meta)
    meta_ref[...] = meta
    metat_ref[0] = meta.T[0:8, :]


def _chunk_rows(cidx, n):
    return pl.ds(cidx, n, stride=ROW_TILE)


def _store_row_tiles(ref, x):
    for cidx in range(ROW_TILE):
        ref[_chunk_rows(cidx, x.shape[0]), :] = x[:, cidx * LANES:(cidx + 1) * LANES]


def _load_row_tiles(ref):
    n = ref.shape[0] // ROW_TILE
    return jnp.concatenate([ref[_chunk_rows(cidx, n), :] for cidx in range(ROW_TILE)], axis=1)


def _tail_in_specs(d, tm, tps):
    return [
        pl.BlockSpec((1, 6, d), lambda i: (i // tps, 0, 0)),
        pl.BlockSpec((1, d), lambda i: (0, 0)),
        pl.BlockSpec((d, 2 * LANES), lambda i: (0, 0)),
        pl.BlockSpec((d, LANES), lambda i: (0, 0)),
        pl.BlockSpec((1, LANES), lambda i: (0, 0)),
        pl.BlockSpec((tm, tm), lambda i: (0, 0)),
    ]


def _tail_out(t, d, tm):
    specs = [
        pl.BlockSpec((tm, d), lambda i: (i, 0)),
        pl.BlockSpec((tm * ROW_TILE, LANES), lambda i: (i, 0)),
        pl.BlockSpec((tm, LANES), lambda i: (i, 0)),
        pl.BlockSpec((1, 8, tm), lambda i: (i, 0, 0)),
        pl.BlockSpec((1, LANES), lambda i: (0, 0)),
    ]
    shapes = [
        jax.ShapeDtypeStruct((t, d), F32),
        jax.ShapeDtypeStruct((t * ROW_TILE, LANES), F32),
        jax.ShapeDtypeStruct((t, LANES), F32),
        jax.ShapeDtypeStruct((t // tm, 8, tm), F32),
        jax.ShapeDtypeStruct((1, LANES), F32),
    ]
    return specs, shapes


def _outproj_kernel(oa_ref, ob_ref, x_ref, wo_ref, mod_ref, nw2_ref, rwhl_ref, rwhi_ref, rb_ref,
                    tril_ref, x1_ref, h2_ref, meta_ref, metat_ref, cnt_ref, carry_ref):
    @pl.when(pl.program_id(0) == 0)
    def _():
        carry_ref[...] = jnp.zeros(carry_ref.shape, F32)

    y = (jnp.dot(oa_ref[...], wo_ref[0:WA, :], preferred_element_type=F32)
         + jnp.dot(ob_ref[...], wo_ref[WA:WA + WB, :], preferred_element_type=F32))
    x1 = x_ref[...] + mod_ref[0, 2:3, :] * y
    x1_ref[...] = x1
    _route_tail(x1, mod_ref, nw2_ref, rwhl_ref, rwhi_ref, rb_ref, tril_ref, carry_ref,
                h2_ref, meta_ref, metat_ref, cnt_ref)


def _outproj_route(oa, ob, x, w_out_b, mod, nw2, rwhl, rwhi, rb, tril, seq, tm):
    t, d = x.shape
    tps = seq // tm
    out_specs, out_shapes = _tail_out(t, d, tm)
    return pl.pallas_call(
        _outproj_kernel,
        grid=(t // tm,),
        in_specs=[
            pl.BlockSpec((tm, WA), lambda i: (i, 0)),
            pl.BlockSpec((tm, WB), lambda i: (i, 0)),
            pl.BlockSpec((tm, d), lambda i: (i, 0)),
            pl.BlockSpec((WA + WB, d), lambda i: (0, 0)),
        ] + _tail_in_specs(d, tm, tps),
        out_specs=out_specs,
        out_shape=out_shapes,
        scratch_shapes=[pltpu.VMEM((1, LANES), F32)],
        compiler_params=_cparams(("arbitrary",)),
        name="outproj_route",
    )(oa, ob, x, w_out_b, mod, nw2, rwhl, rwhi, rb, tril)


def _pool_kernel(x_ref, nw1_ref, pw_ref, ps_ref, mod_ref, nw2_ref, rwhl_ref, rwhi_ref, rb_ref,
                 tril_ref, x1_ref, h2_ref, meta_ref, metat_ref, cnt_ref, carry_ref, e1_ref, e2_ref, e3_ref,
                 *, blocks_per_seq):
    tp, d = x_ref.shape
    hist = POOL_HIST
    i = pl.program_id(0)

    @pl.when(i == 0)
    def _():
        carry_ref[...] = jnp.zeros(carry_ref.shape, F32)

    @pl.when(i % blocks_per_seq == 0)
    def _():
        e1_ref[0:hist, :] = jnp.zeros((hist, d), F32)

    x = x_ref[...]
    h = _norm_mod(x, nw1_ref[...], mod_ref[0, 1:2, :], mod_ref[0, 0:1, :])
    e1_ref[hist:hist + tp, :] = h
    pos = (i % blocks_per_seq) * tp + lax.broadcasted_iota(I32, (tp, 1), 0)
    gdim = POOL_GROUP_DIM
    ys = []
    for gi, win in enumerate(POOL_WINDOWS):
        c0 = gi * gdim
        levels = win.bit_length() - 1
        src, spare = e1_ref, (e2_ref, e3_ref)
        for m in range(1, levels + 1):
            shift = 1 << (m - 1)
            start = hist - 8 * (levels - m)
            n = hist + tp - start
            cur = src[start:start + n, c0:c0 + gdim] + src[start - shift:start - shift + n, c0:c0 + gdim]
            if m < levels:
                dst = spare[m % 2]
                dst[start:start + n, c0:c0 + gdim] = cur
                src = dst
        cnt = jnp.minimum(pos + 1, win).astype(F32)
        pooled = cur / cnt - h[:, c0:c0 + gdim]
        ys.append(_mm(pooled, pw_ref[gi]))
    e1_ref[0:hist, :] = h[tp - hist:tp, :]
    y = jnp.concatenate(ys, axis=1) * ps_ref[...]
    x1 = x + mod_ref[0, 2:3, :] * y
    x1_ref[...] = x1
    _route_tail(x1, mod_ref, nw2_ref, rwhl_ref, rwhi_ref, rb_ref, tril_ref, carry_ref,
                h2_ref, meta_ref, metat_ref, cnt_ref)


def _pool_route(x, nw1, pool_w_b, pool_scale, mod, nw2, rwhl, rwhi, rb, tril, seq, tp):
    t, d = x.shape
    tps = seq // tp
    out_specs, out_shapes = _tail_out(t, d, tp)
    kern = functools.partial(_pool_kernel, blocks_per_seq=tps)
    return pl.pallas_call(
        kern,
        grid=(t // tp,),
        in_specs=[
            pl.BlockSpec((tp, d), lambda i: (i, 0)),
            pl.BlockSpec((1, d), lambda i: (0, 0)),
            pl.BlockSpec((4, POOL_GROUP_DIM, POOL_GROUP_DIM), lambda i: (0, 0, 0)),
            pl.BlockSpec((1, d), lambda i: (0, 0)),
        ] + _tail_in_specs(d, tp, tps),
        out_specs=out_specs,
        out_shape=out_shapes,
        scratch_shapes=[
            pltpu.VMEM((1, LANES), F32),
            pltpu.VMEM((tp + POOL_HIST, d), F32),
            pltpu.VMEM((tp + POOL_HIST, d), F32),
            pltpu.VMEM((tp + POOL_HIST, d), F32),
        ],
        compiler_params=_cparams(("arbitrary",)),
        name="pool_route",
    )(x, nw1, pool_w_b, pool_scale, mod, nw2, rwhl, rwhi, rb, tril)


def _dispatch_kernel(dest_ref, fill_ref, h_ref, xs_ref, zero_ref, sem):
    td = h_ref.shape[0] // ROW_TILE

    def row_tile(r, n=1):
        return pl.ds(pl.multiple_of(r * ROW_TILE, ROW_TILE), n * ROW_TILE)

    @pl.when(pl.program_id(0) == 0)
    def _():
        zero_ref[...] = jnp.zeros(zero_ref.shape, zero_ref.dtype)

        def fill(e, carry):
            pos = fill_ref[0, e]
            n = fill_ref[1, e]
            sizes = [1 << bit for bit in reversed(range(MOE_ROWS.bit_length() - 1))]
            for size in sizes:
                take = (n & size) != 0

                @pl.when(take)
                def _():
                    pltpu.make_async_copy(zero_ref.at[row_tile(0, size)], xs_ref.at[row_tile(pos, size)], sem).start()

                pos = pos + jnp.where(take, size, 0)
            for size in sizes:
                @pl.when((n & size) != 0)
                def _():
                    pltpu.make_async_copy(zero_ref.at[row_tile(0, size)], xs_ref.at[row_tile(0, size)], sem).wait()

            return carry

        lax.fori_loop(0, N_EXPERTS, fill, 0)

        half = MOE_ROWS // 2

        def fill_tail(b, carry):
            for part in range(2):
                pltpu.make_async_copy(zero_ref, xs_ref.at[row_tile(b * MOE_ROWS + part * half, half)], sem).start()
            for part in range(2):
                pltpu.make_async_copy(zero_ref, xs_ref.at[row_tile(0, half)], sem).wait()
            return carry

        lax.fori_loop(fill_ref[0, N_EXPERTS], xs_ref.shape[0] // (ROW_TILE * MOE_ROWS), fill_tail, 0)

    def issue(tk, carry):
        for s in range(2):
            d = dest_ref[0, 0, s * td + tk]
            pltpu.make_async_copy(h_ref.at[row_tile(tk)], xs_ref.at[row_tile(d)], sem).start(priority=s)
        return carry

    lax.fori_loop(0, td, issue, 0)

    for s in range(2):
        pltpu.make_async_copy(h_ref, xs_ref.at[row_tile(0, td)], sem).wait()


def _dispatch(dest3, fill, h2, n_rows, td):
    t = h2.shape[0] // ROW_TILE
    return pl.pallas_call(
        _dispatch_kernel,
        grid=(t // td,),
        in_specs=[
            pl.BlockSpec((1, 1, 2 * td), lambda i: (i, 0, 0), memory_space=pltpu.SMEM),
            pl.BlockSpec(memory_space=pltpu.SMEM),
            pl.BlockSpec((td * ROW_TILE, LANES), lambda i: (i, 0)),
        ],
        out_specs=pl.BlockSpec(memory_space=pl.ANY),
        out_shape=jax.ShapeDtypeStruct((n_rows * ROW_TILE, LANES), h2.dtype),
        scratch_shapes=[pltpu.VMEM((MOE_ROWS // 2 * ROW_TILE, LANES), h2.dtype), pltpu.SemaphoreType.DMA(())],
        compiler_params=_cparams(("arbitrary",)),
        name="moe_dispatch",
    )(dest3, fill, h2)


def _expert_kernel(be_ref, nu_ref, xs_ref, wg_ref, wu_ref, wd_ref, y_ref, wgb_ref, wub_ref, wdb_ref):
    i = pl.program_id(0)
    prev = be_ref[jnp.maximum(i - 1, 0)]
    changed = (i == 0) | (be_ref[i] != prev)

    @pl.when(changed)
    def _():
        wgb_ref[...] = wg_ref[0, 0].astype(BF16)
        wub_ref[...] = wu_ref[0, 0].astype(BF16)
        wdb_ref[...] = wd_ref[0, 0].astype(BF16)

    @pl.when(i < nu_ref[0])
    def _():
        xb = _load_row_tiles(xs_ref).astype(BF16)
        g = jnp.dot(xb, wgb_ref[...], preferred_element_type=F32)
        u = jnp.dot(xb, wub_ref[...], preferred_element_type=F32)
        a = (_silu(g) * u).astype(BF16)
        _store_row_tiles(y_ref, jnp.dot(a, wdb_ref[...], preferred_element_type=F32))

    @pl.when(i >= nu_ref[0])
    def _():
        y_ref[...] = jnp.zeros(y_ref.shape, F32)


def _experts(blk_expert, n_used, xs, w_gate, w_up, w_down, layer):
    n_rows = xs.shape[0] // ROW_TILE
    d = ROW_TILE * LANES
    blk = (MOE_ROWS * ROW_TILE, LANES)
    n_blk = n_rows // MOE_ROWS
    return pl.pallas_call(
        _expert_kernel,
        grid_spec=pltpu.PrefetchScalarGridSpec(
            num_scalar_prefetch=2,
            grid=(n_blk,),
            in_specs=[
                pl.BlockSpec(blk, lambda i, be, nu: (i, 0)),
                pl.BlockSpec((1, 1, d, D_FF), lambda i, be, nu: (layer, be[i], 0, 0)),
                pl.BlockSpec((1, 1, d, D_FF), lambda i, be, nu: (layer, be[i], 0, 0)),
                pl.BlockSpec((1, 1, D_FF, d), lambda i, be, nu: (layer, be[i], 0, 0)),
            ],
            out_specs=pl.BlockSpec(blk, lambda i, be, nu: (i, 0)),
            scratch_shapes=[
                pltpu.VMEM((d, D_FF), BF16),
                pltpu.VMEM((d, D_FF), BF16),
                pltpu.VMEM((D_FF, d), BF16),
            ],
        ),
        out_shape=jax.ShapeDtypeStruct(xs.shape, F32),
        compiler_params=_cparams(("arbitrary",)),
        name="moe_experts",
    )(blk_expert, n_used, xs, w_gate, w_up, w_down)


def _combine_kernel(dest_ref, ys_ref, x1_ref, meta_ref, mod_ref, fw_ref, o_ref, buf_ref, sems, *, final_norm, tc):
    n_sub = x1_ref.shape[0] // tc

    def row_tile(r):
        return pl.ds(pl.multiple_of(r * ROW_TILE, ROW_TILE), ROW_TILE)

    for sub in range(n_sub):
        def issue(tk, carry, sub=sub):
            for s in range(2):
                d = dest_ref[sub, 0, s * tc + tk]
                pltpu.make_async_copy(ys_ref.at[row_tile(d)], buf_ref.at[sub, s, row_tile(tk)],
                                      sems.at[sub]).start(priority=s)
            return carry

        lax.fori_loop(0, tc, issue, 0)

    n_chunks = ROW_TILE
    for sub in range(n_sub):
        for s in range(2):
            pltpu.make_async_copy(ys_ref.at[pl.ds(0, tc * ROW_TILE)], buf_ref.at[sub, s], sems.at[sub]).wait()
        r0 = sub * tc
        meta = meta_ref[r0:r0 + tc, :]
        g1 = meta[:, M_G1:M_G1 + 1]
        g2 = meta[:, M_G2:M_G2 + 1]
        ssq = jnp.zeros((tc, 1), F32)
        for cidx in range(n_chunks):
            sl = slice(cidx * LANES, (cidx + 1) * LANES)
            rows = _chunk_rows(cidx, tc)
            moe = buf_ref[sub, 0, rows, :] * g1 + buf_ref[sub, 1, rows, :] * g2
            x2 = x1_ref[r0:r0 + tc, sl] + mod_ref[0, 5:6, sl] * moe
            o_ref[r0:r0 + tc, sl] = x2
            if final_norm:
                ssq = ssq + jnp.sum(x2 * x2, axis=-1, keepdims=True)
        if final_norm:
            inv = lax.rsqrt(ssq * (1.0 / (n_chunks * LANES)) + EPS)
            for cidx in range(n_chunks):
                sl = slice(cidx * LANES, (cidx + 1) * LANES)
                o_ref[r0:r0 + tc, sl] = o_ref[r0:r0 + tc, sl] * inv * fw_ref[:, sl]


def _sub_tiles(seq, tile):
    return 2 if seq % (2 * tile) == 0 else 1


def _combine(dest3, ys, x1, meta, mod, fw, seq, tc, final_norm):
    t, d = x1.shape
    n_sub = _sub_tiles(seq, tc)
    rows = n_sub * tc
    tps = seq // rows
    kern = functools.partial(_combine_kernel, final_norm=final_norm, tc=tc)
    return pl.pallas_call(
        kern,
        grid=(t // rows,),
        in_specs=[
            pl.BlockSpec((n_sub, 1, 2 * tc), lambda i: (i, 0, 0), memory_space=pltpu.SMEM),
            pl.BlockSpec(memory_space=pl.ANY),
            pl.BlockSpec((rows, d), lambda i: (i, 0)),
            pl.BlockSpec((rows, LANES), lambda i: (i, 0)),
            pl.BlockSpec((1, 6, d), lambda i: (i // tps, 0, 0)),
            pl.BlockSpec((1, d), lambda i: (0, 0)),
        ],
        out_specs=pl.BlockSpec((rows, d), lambda i: (i, 0)),
        out_shape=jax.ShapeDtypeStruct((t, d), F32),
        scratch_shapes=[pltpu.VMEM((n_sub, 2, tc * ROW_TILE, LANES), F32), pltpu.SemaphoreType.DMA((n_sub,))],
        compiler_params=_cparams(("arbitrary",)),
        name="moe_combine",
    )(dest3, ys, x1, meta, mod, fw)


def _moe(x1, h2, meta, metat, cnt, mod, fw, w_gate, w_up, w_down, layer, seq, tile, final_norm):
    t, d = x1.shape
    n_rows = -(-(2 * t + N_EXPERTS * (MOE_ROWS - 1)) // MOE_ROWS) * MOE_ROWS
    n_blk = n_rows // MOE_ROWS
    counts = cnt[0, :N_EXPERTS].astype(I32)
    padded = (counts + MOE_ROWS - 1) // MOE_ROWS * MOE_ROWS
    pad_end = jnp.cumsum(padded)
    pad_start = pad_end - padded
    expert = metat[:, M_E1:M_E2 + 1, :].astype(I32)
    rank = metat[:, M_R1:M_R2 + 1, :].astype(I32)
    start_of = jnp.sum(jnp.where(expert[..., None] == jnp.arange(N_EXPERTS, dtype=I32), pad_start, 0), axis=-1)
    dest3 = (start_of + rank).reshape(t // tile, 1, 2 * tile)
    blk_row0 = jnp.arange(n_blk, dtype=I32) * MOE_ROWS
    blk_expert = jnp.minimum(jnp.sum((pad_end[None, :] <= blk_row0[:, None]).astype(I32), axis=1),
                             N_EXPERTS - 1)
    n_used = (pad_end[-1:] // MOE_ROWS).astype(I32)
    fill = jnp.stack([jnp.concatenate([pad_start + counts, n_used]),
                      jnp.concatenate([padded - counts, jnp.zeros((1,), I32)])])
    xs = _dispatch(dest3, fill, h2, n_rows, tile)
    ys = _experts(blk_expert, n_used, xs, w_gate, w_up, w_down, layer)
    return _combine(dest3, ys, x1, meta, mod, fw, seq, tile, final_norm)


def _router_params(r1_w, r1_b, r2_w, r2_b):
    d = r1_w.shape[0]
    rw = jnp.zeros((d, LANES), F32).at[:, :N_EXPERTS].set(r2_w).at[:, R_GRP0:R_GRP0 + N_GROUPS].set(r1_w)
    rb = jnp.zeros((1, LANES), F32).at[0, :N_EXPERTS].set(r2_b).at[0, R_GRP0:R_GRP0 + N_GROUPS].set(r1_b)
    hi = rw.astype(BF16)
    lo = (rw - hi.astype(F32)).astype(BF16)
    return jnp.concatenate([hi, lo], axis=1), hi, rb


def kernel(x, c, ada_w, ada_b, norm_mix_w, norm_ffn_w, ab_w_in, ab_conv_w, ab_a_log, ab_dt_bias, ab_onorm_w, ab_sinks, rel_bias, ab_w_out, pool_w, pool_scale, r1_w, r1_b, r2_w, r2_b, moe_w_gate, moe_w_up, moe_w_down, final_norm_w):
    bsz, seq, d = x.shape
    t = bsz * seq
    tile = min(512, seq)
    lb = min(256, seq)
    xf = x.reshape(t, d)

    mod_all = _ada(c, ada_w, ada_b).reshape(DEPTH, bsz, 6, d)
    tril = jnp.tril(jnp.ones((tile, tile), BF16))
    fw = final_norm_w.reshape(1, d)

    w_in = ab_w_in[0]
    cuts = (0, WA, 2 * WA, 3 * WA, 4 * WA, 4 * WA + H_A, 4 * WA + 2 * H_A)
    qkvz = w_in[:, :cuts[4]]
    ba = w_in[:, cuts[4]:cuts[6]]
    qkv_b = w_in[:, cuts[6]:]
    w_all = jnp.concatenate([qkvz, qkv_b, ba, jnp.zeros((d, LANES - 2 * H_A), F32)], axis=1).astype(BF16)
    a1, b1, g1 = _inproj(xf, mod_all[0], norm_mix_w[0:1], w_all, seq, tile)

    pad = jnp.zeros((LANES - 2 * H_A,), F32)
    alog_row = jnp.concatenate([jnp.zeros((H_A,), F32), ab_a_log[0], pad]).reshape(1, LANES)
    dt_row = jnp.concatenate([jnp.zeros((H_A,), F32), ab_dt_bias[0], pad]).reshape(1, LANES)
    onw_row = jnp.concatenate([ab_onorm_w[0], ab_onorm_w[0]]).reshape(1, LANES)
    oa = _deltanet(a1, g1, ab_conv_w[0], alog_row, dt_row, onw_row, seq, lb)
    ob = _swa(b1, ab_sinks[0], _t5_bias_table(rel_bias), seq)

    rwhl, rwhi, rb = _router_params(r1_w[0], r1_b[0], r2_w[0], r2_b[0])
    x1, h2, meta, metat, cnt = _outproj_route(oa, ob, xf, ab_w_out[0].astype(BF16), mod_all[0], norm_ffn_w[0:1],
                                       rwhl, rwhi, rb, tril, seq, tile)
    x2 = _moe(x1, h2, meta, metat, cnt, mod_all[0], fw, moe_w_gate, moe_w_up, moe_w_down, 0,
              seq, tile, False)

    rwhl, rwhi, rb = _router_params(r1_w[1], r1_b[1], r2_w[1], r2_b[1])
    x3, h2, meta, metat, cnt = _pool_route(x2, norm_mix_w[1:2], pool_w[0].astype(BF16), pool_scale[0].reshape(1, d),
                                    mod_all[1], norm_ffn_w[1:2], rwhl, rwhi, rb, tril, seq, tile)
    out = _moe(x3, h2, meta, metat, cnt, mod_all[1], fw, moe_w_gate, moe_w_up, moe_w_down, 1,
               seq, tile, True)
    return out.reshape(bsz, seq, d)
```

```python
import functools
import math

import jax
import jax.numpy as jnp
from jax import lax
from jax.experimental import pallas as pl
from jax.experimental.pallas import tpu as pltpu

F32 = jnp.float32
BF16 = jnp.bfloat16
I32 = jnp.int32

D_MODEL = 1024
DEPTH = 2
H_A = 8
DK_A = 64
CONV_K = 4
CHUNK = 64
H_B = 8
H_B_KV = 2
DH_B = 64
WINDOW = 128
ATT_BLK = 128
NUM_BUCKETS = 32
MAX_DISTANCE = 128
POOL_WINDOWS = (2, 4, 8, 16)
POOL_GROUP_DIM = D_MODEL // 4
POOL_HIST = 32
N_GROUPS = 4
EXPERTS_PER_GROUP = 8
N_EXPERTS = N_GROUPS * EXPERTS_PER_GROUP
D_FF = D_MODEL // 2
EPS = 1e-6

WA = H_A * DK_A
WB = H_B * DH_B
WKV = H_B_KV * DH_B
N_PAIR = H_A // 2
LANES = 128
ROW_TILE = D_MODEL // LANES
MOE_ROWS = 512
VMEM_LIMIT = 56 * 1024 * 1024

M_E1, M_E2, M_R1, M_R2, M_G1, M_G2 = 0, 1, 2, 3, 4, 5
R_GRP0 = N_EXPERTS


def _cparams(sem):
    return pltpu.CompilerParams(dimension_semantics=sem, vmem_limit_bytes=VMEM_LIMIT)


def _mm(a, b):
    return jnp.dot(a.astype(BF16), b.astype(BF16), preferred_element_type=F32)


def _mm_nt(a, b):
    return lax.dot_general(a.astype(BF16), b.astype(BF16), (((1,), (1,)), ((), ())),
                           preferred_element_type=F32)


def _mm_x3(a, b):
    ah = a.astype(BF16)
    al = (a - ah.astype(F32)).astype(BF16)
    bh = b.astype(BF16)
    bl = (b - bh.astype(F32)).astype(BF16)
    return jnp.dot(jnp.concatenate([ah, al, ah], axis=1), jnp.concatenate([bh, bh, bl], axis=0),
                   preferred_element_type=F32)


def _split3(x):
    hi = x.astype(BF16)
    r = x - hi.astype(F32)
    mid = r.astype(BF16)
    lo = (r - mid.astype(F32)).astype(BF16)
    return hi, mid, lo


def _norm_mod(x, nw, sc, sh):
    ms = jnp.mean(x * x, axis=-1, keepdims=True)
    return (x * lax.rsqrt(ms + EPS) * nw) * (1.0 + sc) + sh


def _silu(x):
    return x * jax.nn.sigmoid(x)


def _ada_kernel(ct_ref, w_ref, b_ref, o_ref):
    nb = ct_ref.shape[1]
    d = ct_ref.shape[0]
    rows = []
    for b in range(nb):
        acc = None
        for k0 in range(0, d, LANES):
            ct = ct_ref[k0:k0 + LANES, b:b + 1]
            part = jnp.sum(_silu(ct) * w_ref[0, k0:k0 + LANES, :], axis=0, keepdims=True)
            acc = part if acc is None else acc + part
        rows.append(acc)
    o_ref[0] = jnp.concatenate(rows, axis=0) + b_ref[0]


def _ada(c, ada_w, ada_b):
    nb, d = c.shape
    depth, _, n6 = ada_w.shape
    cols = 1536
    return pl.pallas_call(
        _ada_kernel,
        grid=(depth, n6 // cols),
        in_specs=[
            pl.BlockSpec((d, nb), lambda l, j: (0, 0)),
            pl.BlockSpec((1, d, cols), lambda l, j: (l, 0, j)),
            pl.BlockSpec((1, 1, cols), lambda l, j: (l, 0, j)),
        ],
        out_specs=pl.BlockSpec((1, nb, cols), lambda l, j: (l, 0, j)),
        out_shape=jax.ShapeDtypeStruct((depth, nb, n6), F32),
        compiler_params=_cparams(("arbitrary", "arbitrary")),
        name="ada_mod",
    )(c.T, ada_w, ada_b.reshape(depth, 1, n6))


def _inproj_kernel(x_ref, mod_ref, nw_ref, w_ref, a_ref, b_ref, g_ref):
    h = _norm_mod(x_ref[...], nw_ref[...], mod_ref[0, 1:2, :], mod_ref[0, 0:1, :])
    hb = h.astype(BF16)
    na = a_ref.shape[1]
    nbb = b_ref.shape[1]
    a_ref[...] = jnp.dot(hb, w_ref[:, 0:na], preferred_element_type=F32)
    b_ref[...] = jnp.dot(hb, w_ref[:, na:na + nbb], preferred_element_type=F32).astype(BF16)
    g_ref[...] = jnp.dot(hb, w_ref[:, na + nbb:], preferred_element_type=F32)


def _inproj(x, mod, nw, w_all, seq, tm):
    t, d = x.shape
    tps = seq // tm
    na, nbb, ng = 4 * WA, WB + 2 * WKV, LANES
    return pl.pallas_call(
        _inproj_kernel,
        grid=(t // tm,),
        in_specs=[
            pl.BlockSpec((tm, d), lambda i: (i, 0)),
            pl.BlockSpec((1, 6, d), lambda i: (i // tps, 0, 0)),
            pl.BlockSpec((1, d), lambda i: (0, 0)),
            pl.BlockSpec((d, na + nbb + ng), lambda i: (0, 0)),
        ],
        out_specs=[
            pl.BlockSpec((tm, na), lambda i: (i, 0)),
            pl.BlockSpec((tm, nbb), lambda i: (i, 0)),
            pl.BlockSpec((tm, ng), lambda i: (i, 0)),
        ],
        out_shape=[
            jax.ShapeDtypeStruct((t, na), F32),
            jax.ShapeDtypeStruct((t, nbb), BF16),
            jax.ShapeDtypeStruct((t, ng), F32),
        ],
        compiler_params=_cparams(("arbitrary",)),
        name="inproj",
    )(x, mod, nw, w_all)


def _lane_half(shape):
    return lax.broadcasted_iota(I32, shape, 1) < (LANES // 2)


def _half_sums(x2, lo):
    se = jnp.sum(jnp.where(lo, x2, 0.0), axis=-1, keepdims=True)
    so = jnp.sum(jnp.where(lo, 0.0, x2), axis=-1, keepdims=True)
    return jnp.where(lo, se, so)


def _deltanet_kernel(a_ref, gp_ref, cw_ref, alog_ref, dt_ref, onw_ref, o_ref,
                     ext_ref, s_ref, q_ref, k_ref, v_ref, gf_ref, bf_ref, *, blocks_per_seq):
    lb = a_ref.shape[0]
    i = pl.program_id(0)

    @pl.when(i % blocks_per_seq == 0)
    def _():
        ext_ref[0:8, :] = jnp.zeros((8, ext_ref.shape[1]), F32)
        s_ref[...] = jnp.zeros(s_ref.shape, F32)

    lo_b = _lane_half((lb, LANES))
    for part, dst in ((0, q_ref), (1, k_ref), (2, v_ref)):
        for p in range(N_PAIR):
            c0 = part * WA + p * LANES
            xin = a_ref[:, c0:c0 + LANES]
            ext_ref[8:8 + lb, c0:c0 + LANES] = xin
            acc = cw_ref[CONV_K - 1:CONV_K, c0:c0 + LANES] * xin
            for j in range(CONV_K - 1):
                acc = acc + cw_ref[j:j + 1, c0:c0 + LANES] * ext_ref[pl.ds(8 - (CONV_K - 1) + j, lb), c0:c0 + LANES]
            ext_ref[0:8, c0:c0 + LANES] = xin[lb - 8:lb, :]
            y = _silu(acc)
            if part < 2:
                y = y * lax.rsqrt(_half_sums(y * y, lo_b) + EPS)
                if part == 0:
                    y = y * (DK_A ** -0.5)
            dst[:, p * LANES:(p + 1) * LANES] = y

    gp = gp_ref[...]
    bf_ref[...] = jax.nn.sigmoid(gp)
    xg = gp + dt_ref[...]
    softplus = jnp.maximum(xg, 0.0) + jnp.log(1.0 + jnp.exp(-jnp.abs(xg)))
    gf_ref[...] = -jnp.exp(alog_ref[...]) * softplus

    c = CHUNK
    c2 = 2 * c
    row = lax.broadcasted_iota(I32, (c2, c2), 0)
    col = lax.broadcasted_iota(I32, (c2, c2), 1)
    same_blk = (row < c) == (col < c)
    incl = same_blk & (row >= col)
    strict = same_blk & (row > col)
    eye = row == col
    eye_f = jnp.where(eye, 1.0, 0.0)
    lo = _lane_half((c, LANES))

    def stack(x):
        return jnp.where(same_blk, jnp.concatenate([x, x], axis=0), 0.0)

    def unstack(x):
        return jnp.where(lo, x[0:c], x[c:c2])

    rb = lax.broadcasted_iota(I32, (lb, lb), 0)
    cb = lax.broadcasted_iota(I32, (lb, lb), 1)
    shift = CHUNK.bit_length() - 1
    tril_chunks = jnp.where((jnp.right_shift(rb, shift) == jnp.right_shift(cb, shift)) & (rb >= cb),
                            1.0, 0.0).astype(BF16)
    g_hi, g_mid, g_lo = _split3(gf_ref[...])
    gf_ref[...] = (jnp.dot(tril_chunks, g_hi, preferred_element_type=F32)
                   + jnp.dot(tril_chunks, g_mid, preferred_element_type=F32)
                   + jnp.dot(tril_chunks, g_lo, preferred_element_type=F32))

    pairs = range(N_PAIR)

    def chunk_body(ci, carry):
        r0 = pl.multiple_of(ci * c, c)
        gcc = gf_ref[pl.ds(r0, c), :]
        bfc = bf_ref[pl.ds(r0, c), :]
        q = [q_ref[pl.ds(r0, c), p * LANES:(p + 1) * LANES] for p in pairs]
        k = [k_ref[pl.ds(r0, c), p * LANES:(p + 1) * LANES] for p in pairs]
        v = [v_ref[pl.ds(r0, c), p * LANES:(p + 1) * LANES] for p in pairs]

        def lane_bcast(x, j):
            return jnp.broadcast_to(x[:, j:j + 1], (c, LANES))

        beta = [jnp.where(lo, lane_bcast(bfc, 2 * p), lane_bcast(bfc, 2 * p + 1)) for p in pairs]
        gce = [lane_bcast(gcc, H_A + 2 * p) for p in pairs]
        gco = [lane_bcast(gcc, H_A + 2 * p + 1) for p in pairs]
        gcol = [jnp.concatenate([gce[p], gco[p]], axis=0) for p in pairs]
        gc = [jnp.where(lo, gce[p], gco[p]) for p in pairs]
        grow = [jnp.sum(jnp.where(eye, gcol[p], 0.0), axis=0, keepdims=True) for p in pairs]
        decay = [jnp.where(incl, jnp.exp(jnp.where(incl, gcol[p] - grow[p], 0.0)), 0.0) for p in pairs]
        egc = [jnp.exp(gc[p]) for p in pairs]
        kb = [k[p] * beta[p] for p in pairs]
        k_st = [stack(k[p]) for p in pairs]
        n_mat = [jnp.where(strict, _mm_nt(stack(kb[p]), k_st[p]) * decay[p], 0.0) for p in pairs]
        a_qk = [_mm_nt(stack(q[p]), k_st[p]) * decay[p] for p in pairs]
        x_inv = [eye_f - n_mat[p] for p in pairs]
        y_pow = [_mm_x3(n_mat[p], n_mat[p]) for p in pairs]
        for it in range(5):
            x_inv = [x_inv[p] + _mm_x3(x_inv[p], y_pow[p]) for p in pairs]
            if it < 4:
                y_pow = [_mm_x3(y_pow[p], y_pow[p]) for p in pairs]
        sol = []
        for p in pairs:
            vb = v[p] * beta[p]
            kbg = kb[p] * egc[p]
            rhs = jnp.concatenate([jnp.concatenate([vb, vb], axis=0),
                                   jnp.concatenate([kbg, kbg], axis=0)], axis=1)
            sol.append(_mm(x_inv[p], rhs))
        u = [unstack(sol[p][:, 0:LANES]) for p in pairs]
        w = [unstack(sol[p][:, LANES:2 * LANES]) for p in pairs]
        state = [s_ref[p] for p in pairs]
        wq = [_mm(jnp.concatenate([w[p], q[p] * egc[p]], axis=0), state[p]) for p in pairs]
        v_new = [u[p] - wq[p][0:c] for p in pairs]
        av = [_mm(a_qk[p], jnp.concatenate([v_new[p], v_new[p]], axis=0)) for p in pairs]
        for p in pairs:
            k_dec = k[p] * jnp.exp(gc[p][c - 1:c, :] - gc[p])
            g_tot = jnp.where(row < c, jnp.exp(gcol[p][c - 1:c, :]), jnp.exp(gcol[p][c2 - 1:c2, :]))
            s_ref[p] = state[p] * g_tot + jnp.where(same_blk, _mm(k_dec.T, v_new[p]), 0.0)
        for p in pairs:
            out = wq[p][c:c2] + unstack(av[p])
            ms = _half_sums(out * out, lo) * (1.0 / DK_A)
            z = a_ref[pl.ds(r0, c), 3 * WA + p * LANES:3 * WA + (p + 1) * LANES]
            res = out * lax.rsqrt(ms + EPS) * onw_ref[...] * _silu(z)
            o_ref[pl.ds(r0, c), p * LANES:(p + 1) * LANES] = res.astype(BF16)
        return carry

    lax.fori_loop(0, lb // c, chunk_body, 0, unroll=True)


def _deltanet(a1, g1, conv_w, alog_row, dt_row, onw_row, seq, lb):
    t = a1.shape[0]
    kern = functools.partial(_deltanet_kernel, blocks_per_seq=seq // lb)
    return pl.pallas_call(
        kern,
        grid=(t // lb,),
        in_specs=[
            pl.BlockSpec((lb, 4 * WA), lambda i: (i, 0)),
            pl.BlockSpec((lb, LANES), lambda i: (i, 0)),
            pl.BlockSpec((CONV_K, 3 * WA), lambda i: (0, 0)),
            pl.BlockSpec((1, LANES), lambda i: (0, 0)),
            pl.BlockSpec((1, LANES), lambda i: (0, 0)),
            pl.BlockSpec((1, LANES), lambda i: (0, 0)),
        ],
        out_specs=pl.BlockSpec((lb, WA), lambda i: (i, 0)),
        out_shape=jax.ShapeDtypeStruct((t, WA), BF16),
        scratch_shapes=[
            pltpu.VMEM((lb + 8, 3 * WA), F32),
            pltpu.VMEM((N_PAIR, LANES, LANES), F32),
            pltpu.VMEM((lb, WA), F32),
            pltpu.VMEM((lb, WA), F32),
            pltpu.VMEM((lb, WA), F32),
            pltpu.VMEM((lb, LANES), F32),
            pltpu.VMEM((lb, LANES), F32),
        ],
        compiler_params=_cparams(("arbitrary",)),
        name="deltanet",
    )(a1, g1, conv_w, alog_row, dt_row, onw_row)


def _swa_kernel(sink_ref, cur_ref, prev_ref, bias_ref, o_ref, *, blocks_per_seq):
    blk = ATT_BLK
    i = pl.program_id(0)
    first = (i % blocks_per_seq) == 0
    row = lax.broadcasted_iota(I32, (2 * blk, 2 * blk), 0)
    col = lax.broadcasted_iota(I32, (2 * blk, 2 * blk), 1)
    qi = jnp.where(row < blk, row, row - blk)
    dist = qi + blk - col
    first_key = jnp.where(first, blk, 0)
    valid = (dist >= 0) & (dist < WINDOW) & (col >= first_key)
    lo_kv = _lane_half((2 * blk, LANES))
    lo_q = _lane_half((blk, LANES))
    rowq = lax.broadcasted_iota(I32, (2 * blk, LANES), 0)
    q_keep = (rowq < blk) == lo_kv
    top_rows = lax.broadcasted_iota(I32, (2 * blk, 1), 0) < blk

    k0 = WB
    v0 = WB + WKV
    kk = jnp.concatenate([prev_ref[:, k0:k0 + WKV], cur_ref[:, k0:k0 + WKV]], axis=0).astype(F32)
    vv = jnp.concatenate([prev_ref[:, v0:v0 + WKV], cur_ref[:, v0:v0 + WKV]], axis=0).astype(F32)
    kk_sw = pltpu.roll(kk, LANES // 2, axis=1)
    vv_sw = pltpu.roll(vv, LANES // 2, axis=1)
    k_dup = (jnp.where(lo_kv, kk, kk_sw), jnp.where(lo_kv, kk_sw, kk))
    v_dup = (jnp.where(lo_kv, vv, vv_sw), jnp.where(lo_kv, vv_sw, vv))

    for p in range(H_B // 2):
        j = (2 * p) // (H_B // H_B_KV)
        qp = cur_ref[:, p * LANES:(p + 1) * LANES].astype(F32) * (DH_B ** -0.5)
        q_st = jnp.where(q_keep, jnp.concatenate([qp, qp], axis=0), 0.0)
        s = _mm_nt(q_st, k_dup[j])
        s = jnp.where(valid, s + bias_ref[p], -1e30)
        sink = jnp.where(top_rows, sink_ref[2 * p], sink_ref[2 * p + 1])
        m = jnp.maximum(jnp.max(s, axis=-1, keepdims=True), sink)
        pe = jnp.exp(s - m)
        den = jnp.sum(pe, axis=-1, keepdims=True) + jnp.exp(sink - m)
        o2 = _mm(pe, v_dup[j]) / den
        o_ref[:, p * LANES:(p + 1) * LANES] = jnp.where(lo_q, o2[0:blk], o2[blk:2 * blk]).astype(BF16)


def _swa(b1, sinks, bias_tab, seq):
    t, nc = b1.shape
    nbs = seq // ATT_BLK
    kern = functools.partial(_swa_kernel, blocks_per_seq=nbs)
    return pl.pallas_call(
        kern,
        grid=(t // ATT_BLK,),
        in_specs=[
            pl.BlockSpec(memory_space=pltpu.SMEM),
            pl.BlockSpec((ATT_BLK, nc), lambda i: (i, 0)),
            pl.BlockSpec((ATT_BLK, nc), lambda i: (jnp.maximum(i - 1, 0), 0)),
            pl.BlockSpec((H_B // 2, 2 * ATT_BLK, 2 * ATT_BLK), lambda i: (0, 0, 0)),
        ],
        out_specs=pl.BlockSpec((ATT_BLK, WB), lambda i: (i, 0)),
        out_shape=jax.ShapeDtypeStruct((t, WB), BF16),
        compiler_params=_cparams(("arbitrary",)),
        name="swa",
    )(sinks, b1, b1, bias_tab)


def _t5_bias_table(rel_bias):
    qi = jnp.arange(ATT_BLK)[:, None]
    ki = jnp.arange(2 * ATT_BLK)[None, :]
    n = jnp.maximum(qi + ATT_BLK - ki, 0)
    max_exact = NUM_BUCKETS // 2
    nf = jnp.maximum(n, 1).astype(F32)
    large = max_exact + (jnp.log(nf / max_exact) / math.log(MAX_DISTANCE / max_exact)
                         * (NUM_BUCKETS - max_exact)).astype(I32)
    bucket = jnp.where(n < max_exact, n, jnp.minimum(large, NUM_BUCKETS - 1))
    onehot = (bucket[..., None] == jnp.arange(NUM_BUCKETS)).astype(F32)
    bias = jnp.einsum('qkb,bh->hqk', onehot, rel_bias.astype(F32), precision=lax.Precision.HIGHEST)
    return bias.reshape(H_B // 2, 2 * ATT_BLK, 2 * ATT_BLK)


def _route_tail(x1, mod_ref, nw2_ref, rwhl_ref, rwhi_ref, rb_ref, tril_ref, carry_ref,
                h2_ref, meta_ref, metat_ref, cnt_ref):
    tm = x1.shape[0]
    h2 = _norm_mod(x1, nw2_ref[...], mod_ref[0, 4:5, :], mod_ref[0, 3:4, :])
    _store_row_tiles(h2_ref, h2)
    h_hi = h2.astype(BF16)
    h_lo = (h2 - h_hi.astype(F32)).astype(BF16)
    r = jnp.dot(h_hi, rwhl_ref[...], preferred_element_type=F32)
    logits = (r[:, 0:LANES] + r[:, LANES:2 * LANES]
              + jnp.dot(h_lo, rwhi_ref[...], preferred_element_type=F32) + rb_ref[...])
    lane = lax.broadcasted_iota(I32, (tm, LANES), 1)
    lane_f = lane.astype(F32)
    big = 1e9
    neg = -jnp.inf
    is_grp = (lane >= R_GRP0) & (lane < R_GRP0 + N_GROUPS)
    lg1 = jnp.where(is_grp, logits, neg)
    m1 = jnp.max(lg1, axis=-1, keepdims=True)
    grp = jnp.min(jnp.where(lg1 == m1, lane_f, big), axis=-1, keepdims=True) - R_GRP0
    p_top = 1.0 / jnp.sum(jnp.where(is_grp, jnp.exp(lg1 - m1), 0.0), axis=-1, keepdims=True)
    lane_grp = jnp.right_shift(lane, EXPERTS_PER_GROUP.bit_length() - 1).astype(F32)
    in_grp = (lane < N_EXPERTS) & (lane_grp == grp)
    l2 = jnp.where(in_grp, logits, neg)
    t1 = jnp.max(l2, axis=-1, keepdims=True)
    e1 = jnp.min(jnp.where(l2 == t1, lane_f, big), axis=-1, keepdims=True)
    l2b = jnp.where(lane_f == e1, neg, l2)
    t2 = jnp.max(l2b, axis=-1, keepdims=True)
    e2 = jnp.min(jnp.where(l2b == t2, lane_f, big), axis=-1, keepdims=True)
    ex = jnp.exp(t2 - t1)
    gate1 = p_top / (1.0 + ex)
    gate2 = p_top * ex / (1.0 + ex)
    hit1 = lane_f == e1
    hit2 = lane_f == e2
    onehot = jnp.where(hit1 | hit2, 1.0, 0.0)
    incl = jnp.dot(tril_ref[...], onehot.astype(BF16), preferred_element_type=F32)
    excl = incl - onehot + carry_ref[...]
    rank1 = jnp.sum(jnp.where(hit1, excl, 0.0), axis=-1, keepdims=True)
    rank2 = jnp.sum(jnp.where(hit2, excl, 0.0), axis=-1, keepdims=True)
    carry_ref[...] = carry_ref[...] + incl[tm - 1:tm, :]
    cnt_ref[...] = carry_ref[...]
    meta = jnp.zeros((tm, LANES), F32)
    for idx, val in ((M_E1, e1), (M_E2, e2), (M_R1, rank1), (M_R2, rank2), (M_G1, gate1), (M_G2, gate2)):
        meta = jnp.where(lane == idx, val, meta)
    meta_ref[...] = meta
    metat_ref[0] = meta.T[0:8, :]


def _chunk_rows(cidx, n):
    return pl.ds(cidx, n, stride=ROW_TILE)


def _store_row_tiles(ref, x):
    for cidx in range(ROW_TILE):
        ref[_chunk_rows(cidx, x.shape[0]), :] = x[:, cidx * LANES:(cidx + 1) * LANES]


def _load_row_tiles(ref):
    n = ref.shape[0] // ROW_TILE
    return jnp.concatenate([ref[_chunk_rows(cidx, n), :] for cidx in range(ROW_TILE)], axis=1)


def _tail_in_specs(d, tm, tps):
    return [
        pl.BlockSpec((1, 6, d), lambda i: (i // tps, 0, 0)),
        pl.BlockSpec((1, d), lambda i: (0, 0)),
        pl.BlockSpec((d, 2 * LANES), lambda i: (0, 0)),
        pl.BlockSpec((d, LANES), lambda i: (0, 0)),
        pl.BlockSpec((1, LANES), lambda i: (0, 0)),
        pl.BlockSpec((tm, tm), lambda i: (0, 0)),
    ]


def _tail_out(t, d, tm):
    specs = [
        pl.BlockSpec((tm, d), lambda i: (i, 0)),
        pl.BlockSpec((tm * ROW_TILE, LANES), lambda i: (i, 0)),
        pl.BlockSpec((tm, LANES), lambda i: (i, 0)),
        pl.BlockSpec((1, 8, tm), lambda i: (i, 0, 0)),
        pl.BlockSpec((1, LANES), lambda i: (0, 0)),
    ]
    shapes = [
        jax.ShapeDtypeStruct((t, d), F32),
        jax.ShapeDtypeStruct((t * ROW_TILE, LANES), F32),
        jax.ShapeDtypeStruct((t, LANES), F32),
        jax.ShapeDtypeStruct((t // tm, 8, tm), F32),
        jax.ShapeDtypeStruct((1, LANES), F32),
    ]
    return specs, shapes


def _outproj_kernel(oa_ref, ob_ref, x_ref, wo_ref, mod_ref, nw2_ref, rwhl_ref, rwhi_ref, rb_ref,
                    tril_ref, x1_ref, h2_ref, meta_ref, metat_ref, cnt_ref, carry_ref):
    @pl.when(pl.program_id(0) == 0)
    def _():
        carry_ref[...] = jnp.zeros(carry_ref.shape, F32)

    y = (jnp.dot(oa_ref[...], wo_ref[0:WA, :], preferred_element_type=F32)
         + jnp.dot(ob_ref[...], wo_ref[WA:WA + WB, :], preferred_element_type=F32))
    x1 = x_ref[...] + mod_ref[0, 2:3, :] * y
    x1_ref[...] = x1
    _route_tail(x1, mod_ref, nw2_ref, rwhl_ref, rwhi_ref, rb_ref, tril_ref, carry_ref,
                h2_ref, meta_ref, metat_ref, cnt_ref)


def _outproj_route(oa, ob, x, w_out_b, mod, nw2, rwhl, rwhi, rb, tril, seq, tm):
    t, d = x.shape
    tps = seq // tm
    out_specs, out_shapes = _tail_out(t, d, tm)
    return pl.pallas_call(
        _outproj_kernel,
        grid=(t // tm,),
        in_specs=[
            pl.BlockSpec((tm, WA), lambda i: (i, 0)),
            pl.BlockSpec((tm, WB), lambda i: (i, 0)),
            pl.BlockSpec((tm, d), lambda i: (i, 0)),
            pl.BlockSpec((WA + WB, d), lambda i: (0, 0)),
        ] + _tail_in_specs(d, tm, tps),
        out_specs=out_specs,
        out_shape=out_shapes,
        scratch_shapes=[pltpu.VMEM((1, LANES), F32)],
        compiler_params=_cparams(("arbitrary",)),
        name="outproj_route",
    )(oa, ob, x, w_out_b, mod, nw2, rwhl, rwhi, rb, tril)


def _pool_kernel(x_ref, nw1_ref, pw_ref, ps_ref, mod_ref, nw2_ref, rwhl_ref, rwhi_ref, rb_ref,
                 tril_ref, x1_ref, h2_ref, meta_ref, metat_ref, cnt_ref, carry_ref, e1_ref, e2_ref, e3_ref,
                 *, blocks_per_seq):
    tp, d = x_ref.shape
    hist = POOL_HIST
    i = pl.program_id(0)

    @pl.when(i == 0)
    def _():
        carry_ref[...] = jnp.zeros(carry_ref.shape, F32)

    @pl.when(i % blocks_per_seq == 0)
    def _():
        e1_ref[0:hist, :] = jnp.zeros((hist, d), F32)

    x = x_ref[...]
    h = _norm_mod(x, nw1_ref[...], mod_ref[0, 1:2, :], mod_ref[0, 0:1, :])
    e1_ref[hist:hist + tp, :] = h
    pos = (i % blocks_per_seq) * tp + lax.broadcasted_iota(I32, (tp, 1), 0)
    gdim = POOL_GROUP_DIM
    ys = []
    for gi, win in enumerate(POOL_WINDOWS):
        c0 = gi * gdim
        levels = win.bit_length() - 1
        src, spare = e1_ref, (e2_ref, e3_ref)
        for m in range(1, levels + 1):
            shift = 1 << (m - 1)
            start = hist - 8 * (levels - m)
            n = hist + tp - start
            cur = src[start:start + n, c0:c0 + gdim] + src[start - shift:start - shift + n, c0:c0 + gdim]
            if m < levels:
                dst = spare[m % 2]
                dst[start:start + n, c0:c0 + gdim] = cur
                src = dst
        cnt = jnp.minimum(pos + 1, win).astype(F32)
        pooled = cur / cnt - h[:, c0:c0 + gdim]
        ys.append(_mm(pooled, pw_ref[gi]))
    e1_ref[0:hist, :] = h[tp - hist:tp, :]
    y = jnp.concatenate(ys, axis=1) * ps_ref[...]
    x1 = x + mod_ref[0, 2:3, :] * y
    x1_ref[...] = x1
    _route_tail(x1, mod_ref, nw2_ref, rwhl_ref, rwhi_ref, rb_ref, tril_ref, carry_ref,
                h2_ref, meta_ref, metat_ref, cnt_ref)


def _pool_route(x, nw1, pool_w_b, pool_scale, mod, nw2, rwhl, rwhi, rb, tril, seq, tp):
    t, d = x.shape
    tps = seq // tp
    out_specs, out_shapes = _tail_out(t, d, tp)
    kern = functools.partial(_pool_kernel, blocks_per_seq=tps)
    return pl.pallas_call(
        kern,
        grid=(t // tp,),
        in_specs=[
            pl.BlockSpec((tp, d), lambda i: (i, 0)),
            pl.BlockSpec((1, d), lambda i: (0, 0)),
            pl.BlockSpec((4, POOL_GROUP_DIM, POOL_GROUP_DIM), lambda i: (0, 0, 0)),
            pl.BlockSpec((1, d), lambda i: (0, 0)),
        ] + _tail_in_specs(d, tp, tps),
        out_specs=out_specs,
        out_shape=out_shapes,
        scratch_shapes=[
            pltpu.VMEM((1, LANES), F32),
            pltpu.VMEM((tp + POOL_HIST, d), F32),
            pltpu.VMEM((tp + POOL_HIST, d), F32),
            pltpu.VMEM((tp + POOL_HIST, d), F32),
        ],
        compiler_params=_cparams(("arbitrary",)),
        name="pool_route",
    )(x, nw1, pool_w_b, pool_scale, mod, nw2, rwhl, rwhi, rb, tril)


def _dispatch_kernel(dest_ref, fill_ref, h_ref, xs_ref, zero_ref, sem):
    td = h_ref.shape[0] // ROW_TILE

    def row_tile(r, n=1):
        return pl.ds(pl.multiple_of(r * ROW_TILE, ROW_TILE), n * ROW_TILE)

    @pl.when(pl.program_id(0) == 0)
    def _():
        zero_ref[...] = jnp.zeros(zero_ref.shape, zero_ref.dtype)

        def fill(e, carry):
            pos = fill_ref[0, e]
            n = fill_ref[1, e]
            sizes = [1 << bit for bit in reversed(range(MOE_ROWS.bit_length() - 1))]
            for size in sizes:
                take = (n & size) != 0

                @pl.when(take)
                def _():
                    pltpu.make_async_copy(zero_ref.at[row_tile(0, size)], xs_ref.at[row_tile(pos, size)], sem).start()

                pos = pos + jnp.where(take, size, 0)
            for size in sizes:
                @pl.when((n & size) != 0)
                def _():
                    pltpu.make_async_copy(zero_ref.at[row_tile(0, size)], xs_ref.at[row_tile(0, size)], sem).wait()

            return carry

        lax.fori_loop(0, N_EXPERTS, fill, 0)

        half = MOE_ROWS // 2

        def fill_tail(b, carry):
            for part in range(2):
                pltpu.make_async_copy(zero_ref, xs_ref.at[row_tile(b * MOE_ROWS + part * half, half)], sem).start()
            return carry

        def wait_tail(b, carry):
            for part in range(2):
                pltpu.make_async_copy(zero_ref, xs_ref.at[row_tile(0, half)], sem).wait()
            return carry

        n_blk = xs_ref.shape[0] // (ROW_TILE * MOE_ROWS)
        lax.fori_loop(fill_ref[0, N_EXPERTS], n_blk, fill_tail, 0)
        lax.fori_loop(fill_ref[0, N_EXPERTS], n_blk, wait_tail, 0)

    def issue(tk, carry):
        for s in range(2):
            d = dest_ref[0, 0, s * td + tk]
            pltpu.make_async_copy(h_ref.at[row_tile(tk)], xs_ref.at[row_tile(d)], sem).start(priority=s)
        return carry

    lax.fori_loop(0, td, issue, 0)

    for s in range(2):
        pltpu.make_async_copy(h_ref, xs_ref.at[row_tile(0, td)], sem).wait()


def _dispatch(dest3, fill, h2, n_rows, td):
    t = h2.shape[0] // ROW_TILE
    return pl.pallas_call(
        _dispatch_kernel,
        grid=(t // td,),
        in_specs=[
            pl.BlockSpec((1, 1, 2 * td), lambda i: (i, 0, 0), memory_space=pltpu.SMEM),
            pl.BlockSpec(memory_space=pltpu.SMEM),
            pl.BlockSpec((td * ROW_TILE, LANES), lambda i: (i, 0)),
        ],
        out_specs=pl.BlockSpec(memory_space=pl.ANY),
        out_shape=jax.ShapeDtypeStruct((n_rows * ROW_TILE, LANES), h2.dtype),
        scratch_shapes=[pltpu.VMEM((MOE_ROWS // 2 * ROW_TILE, LANES), h2.dtype), pltpu.SemaphoreType.DMA(())],
        compiler_params=_cparams(("arbitrary",)),
        name="moe_dispatch",
    )(dest3, fill, h2)


def _expert_kernel(be_ref, nu_ref, xs_ref, wg_ref, wu_ref, wd_ref, y_ref, wgb_ref, wub_ref, wdb_ref):
    i = pl.program_id(0)
    prev = be_ref[jnp.maximum(i - 1, 0)]
    changed = (i == 0) | (be_ref[i] != prev)

    @pl.when(changed)
    def _():
        wgb_ref[...] = wg_ref[0, 0].astype(BF16)
        wub_ref[...] = wu_ref[0, 0].astype(BF16)
        wdb_ref[...] = wd_ref[0, 0].astype(BF16)

    @pl.when(i < nu_ref[0])
    def _():
        xb = _load_row_tiles(xs_ref).astype(BF16)
        g = jnp.dot(xb, wgb_ref[...], preferred_element_type=F32)
        u = jnp.dot(xb, wub_ref[...], preferred_element_type=F32)
        a = (_silu(g) * u).astype(BF16)
        _store_row_tiles(y_ref, jnp.dot(a, wdb_ref[...], preferred_element_type=F32))

    @pl.when(i >= nu_ref[0])
    def _():
        y_ref[...] = jnp.zeros(y_ref.shape, F32)


def _experts(blk_expert, n_used, xs, w_gate, w_up, w_down, layer):
    n_rows = xs.shape[0] // ROW_TILE
    d = ROW_TILE * LANES
    blk = (MOE_ROWS * ROW_TILE, LANES)
    n_blk = n_rows // MOE_ROWS
    return pl.pallas_call(
        _expert_kernel,
        grid_spec=pltpu.PrefetchScalarGridSpec(
            num_scalar_prefetch=2,
            grid=(n_blk,),
            in_specs=[
                pl.BlockSpec(blk, lambda i, be, nu: (i, 0)),
                pl.BlockSpec((1, 1, d, D_FF), lambda i, be, nu: (layer, be[i], 0, 0)),
                pl.BlockSpec((1, 1, d, D_FF), lambda i, be, nu: (layer, be[i], 0, 0)),
                pl.BlockSpec((1, 1, D_FF, d), lambda i, be, nu: (layer, be[i], 0, 0)),
            ],
            out_specs=pl.BlockSpec(blk, lambda i, be, nu: (i, 0)),
            scratch_shapes=[
                pltpu.VMEM((d, D_FF), BF16),
                pltpu.VMEM((d, D_FF), BF16),
                pltpu.VMEM((D_FF, d), BF16),
            ],
        ),
        out_shape=jax.ShapeDtypeStruct(xs.shape, F32),
        compiler_params=_cparams(("arbitrary",)),
        name="moe_experts",
    )(blk_expert, n_used, xs, w_gate, w_up, w_down)


def _combine_kernel(dest_ref, ys_ref, x1_ref, meta_ref, mod_ref, fw_ref, o_ref, buf_ref, sem, *, final_norm):
    tc = x1_ref.shape[0]

    def row_tile(r):
        return pl.ds(pl.multiple_of(r * ROW_TILE, ROW_TILE), ROW_TILE)

    def issue(tk, carry):
        for s in range(2):
            d = dest_ref[0, 0, s * tc + tk]
            pltpu.make_async_copy(ys_ref.at[row_tile(d)], buf_ref.at[s, row_tile(tk)], sem).start(priority=s)
        return carry

    lax.fori_loop(0, tc, issue, 0)

    for s in range(2):
        pltpu.make_async_copy(ys_ref.at[pl.ds(0, tc * ROW_TILE)], buf_ref.at[s], sem).wait()

    meta = meta_ref[...]
    g1 = meta[:, M_G1:M_G1 + 1]
    g2 = meta[:, M_G2:M_G2 + 1]
    n_chunks = ROW_TILE
    ssq = jnp.zeros((tc, 1), F32)
    for cidx in range(n_chunks):
        sl = slice(cidx * LANES, (cidx + 1) * LANES)
        rows = _chunk_rows(cidx, tc)
        moe = buf_ref[0, rows, :] * g1 + buf_ref[1, rows, :] * g2
        x2 = x1_ref[:, sl] + mod_ref[0, 5:6, sl] * moe
        o_ref[:, sl] = x2
        if final_norm:
            ssq = ssq + jnp.sum(x2 * x2, axis=-1, keepdims=True)
    if final_norm:
        inv = lax.rsqrt(ssq * (1.0 / (n_chunks * LANES)) + EPS)
        for cidx in range(n_chunks):
            sl = slice(cidx * LANES, (cidx + 1) * LANES)
            o_ref[:, sl] = o_ref[:, sl] * inv * fw_ref[:, sl]


def _combine(dest3, ys, x1, meta, mod, fw, seq, tc, final_norm):
    t, d = x1.shape
    tps = seq // tc
    kern = functools.partial(_combine_kernel, final_norm=final_norm)
    return pl.pallas_call(
        kern,
        grid=(t // tc,),
        in_specs=[
            pl.BlockSpec((1, 1, 2 * tc), lambda i: (i, 0, 0), memory_space=pltpu.SMEM),
            pl.BlockSpec(memory_space=pl.ANY),
            pl.BlockSpec((tc, d), lambda i: (i, 0)),
            pl.BlockSpec((tc, LANES), lambda i: (i, 0)),
            pl.BlockSpec((1, 6, d), lambda i: (i // tps, 0, 0)),
            pl.BlockSpec((1, d), lambda i: (0, 0)),
        ],
        out_specs=pl.BlockSpec((tc, d), lambda i: (i, 0)),
        out_shape=jax.ShapeDtypeStruct((t, d), F32),
        scratch_shapes=[pltpu.VMEM((2, tc * ROW_TILE, LANES), F32), pltpu.SemaphoreType.DMA(())],
        compiler_params=_cparams(("arbitrary",)),
        name="moe_combine",
    )(dest3, ys, x1, meta, mod, fw)


def _moe(x1, h2, meta, metat, cnt, mod, fw, w_gate, w_up, w_down, layer, seq, tile, final_norm):
    t, d = x1.shape
    n_rows = -(-(2 * t + N_EXPERTS * (MOE_ROWS - 1)) // MOE_ROWS) * MOE_ROWS
    n_blk = n_rows // MOE_ROWS
    counts = cnt[0, :N_EXPERTS].astype(I32)
    padded = (counts + MOE_ROWS - 1) // MOE_ROWS * MOE_ROWS
    pad_end = jnp.cumsum(padded)
    pad_start = pad_end - padded
    expert = metat[:, M_E1:M_E2 + 1, :].astype(I32)
    rank = metat[:, M_R1:M_R2 + 1, :].astype(I32)
    start_of = jnp.sum(jnp.where(expert[..., None] == jnp.arange(N_EXPERTS, dtype=I32), pad_start, 0), axis=-1)
    dest3 = (start_of + rank).reshape(t // tile, 1, 2 * tile)
    blk_row0 = jnp.arange(n_blk, dtype=I32) * MOE_ROWS
    blk_expert = jnp.minimum(jnp.sum((pad_end[None, :] <= blk_row0[:, None]).astype(I32), axis=1),
                             N_EXPERTS - 1)
    n_used = (pad_end[-1:] // MOE_ROWS).astype(I32)
    fill = jnp.stack([jnp.concatenate([pad_start + counts, n_used]),
                      jnp.concatenate([padded - counts, jnp.zeros((1,), I32)])])
    xs = _dispatch(dest3, fill, h2, n_rows, tile)
    ys = _experts(blk_expert, n_used, xs, w_gate, w_up, w_down, layer)
    return _combine(dest3, ys, x1, meta, mod, fw, seq, tile, final_norm)


def _router_params(r1_w, r1_b, r2_w, r2_b):
    d = r1_w.shape[0]
    rw = jnp.zeros((d, LANES), F32).at[:, :N_EXPERTS].set(r2_w).at[:, R_GRP0:R_GRP0 + N_GROUPS].set(r1_w)
    rb = jnp.zeros((1, LANES), F32).at[0, :N_EXPERTS].set(r2_b).at[0, R_GRP0:R_GRP0 + N_GROUPS].set(r1_b)
    hi = rw.astype(BF16)
    lo = (rw - hi.astype(F32)).astype(BF16)
    return jnp.concatenate([hi, lo], axis=1), hi, rb


def kernel(x, c, ada_w, ada_b, norm_mix_w, norm_ffn_w, ab_w_in, ab_conv_w, ab_a_log, ab_dt_bias, ab_onorm_w, ab_sinks, rel_bias, ab_w_out, pool_w, pool_scale, r1_w, r1_b, r2_w, r2_b, moe_w_gate, moe_w_up, moe_w_down, final_norm_w):
    bsz, seq, d = x.shape
    t = bsz * seq
    tile = min(512, seq)
    lb = min(256, seq)
    xf = x.reshape(t, d)

    mod_all = _ada(c, ada_w, ada_b).reshape(DEPTH, bsz, 6, d)
    tril = jnp.tril(jnp.ones((tile, tile), BF16))
    fw = final_norm_w.reshape(1, d)

    w_in = ab_w_in[0]
    cuts = (0, WA, 2 * WA, 3 * WA, 4 * WA, 4 * WA + H_A, 4 * WA + 2 * H_A)
    qkvz = w_in[:, :cuts[4]]
    ba = w_in[:, cuts[4]:cuts[6]]
    qkv_b = w_in[:, cuts[6]:]
    w_all = jnp.concatenate([qkvz, qkv_b, ba, jnp.zeros((d, LANES - 2 * H_A), F32)], axis=1).astype(BF16)
    a1, b1, g1 = _inproj(xf, mod_all[0], norm_mix_w[0:1], w_all, seq, tile)

    pad = jnp.zeros((LANES - 2 * H_A,), F32)
    alog_row = jnp.concatenate([jnp.zeros((H_A,), F32), ab_a_log[0], pad]).reshape(1, LANES)
    dt_row = jnp.concatenate([jnp.zeros((H_A,), F32), ab_dt_bias[0], pad]).reshape(1, LANES)
    onw_row = jnp.concatenate([ab_onorm_w[0], ab_onorm_w[0]]).reshape(1, LANES)
    oa = _deltanet(a1, g1, ab_conv_w[0], alog_row, dt_row, onw_row, seq, lb)
    ob = _swa(b1, ab_sinks[0], _t5_bias_table(rel_bias), seq)

    rwhl, rwhi, rb = _router_params(r1_w[0], r1_b[0], r2_w[0], r2_b[0])
    x1, h2, meta, metat, cnt = _outproj_route(oa, ob, xf, ab_w_out[0].astype(BF16), mod_all[0], norm_ffn_w[0:1],
                                       rwhl, rwhi, rb, tril, seq, tile)
    x2 = _moe(x1, h2, meta, metat, cnt, mod_all[0], fw, moe_w_gate, moe_w_up, moe_w_down, 0,
              seq, tile, False)

    rwhl, rwhi, rb = _router_params(r1_w[1], r1_b[1], r2_w[1], r2_b[1])
    x3, h2, meta, metat, cnt = _pool_route(x2, norm_mix_w[1:2], pool_w[0].astype(BF16), pool_scale[0].reshape(1, d),
                                    mod_all[1], norm_ffn_w[1:2], rwhl, rwhi, rb, tril, seq, tile)
    out = _moe(x3, h2, meta, metat, cnt, mod_all[1], fw, moe_w_gate, moe_w_up, moe_w_down, 1,
               seq, tile, True)
    return out.reshape(bsz, seq, d)
```

```python
import functools
import math

import jax
import jax.numpy as jnp
from jax import lax
from jax.experimental import pallas as pl
from jax.experimental.pallas import tpu as pltpu

F32 = jnp.float32
BF16 = jnp.bfloat16
I32 = jnp.int32

D_MODEL = 1024
DEPTH = 2
H_A = 8
DK_A = 64
CONV_K = 4
CHUNK = 64
H_B = 8
H_B_KV = 2
DH_B = 64
WINDOW = 128
ATT_BLK = 128
NUM_BUCKETS = 32
MAX_DISTANCE = 128
POOL_WINDOWS = (2, 4, 8, 16)
POOL_GROUP_DIM = D_MODEL // 4
POOL_HIST = 32
N_GROUPS = 4
EXPERTS_PER_GROUP = 8
N_EXPERTS = N_GROUPS * EXPERTS_PER_GROUP
D_FF = D_MODEL // 2
EPS = 1e-6

WA = H_A * DK_A
WB = H_B * DH_B
WKV = H_B_KV * DH_B
N_PAIR = H_A // 2
LANES = 128
ROW_TILE = D_MODEL // LANES
XS_RING = 3
MOE_ROWS = 512
VMEM_LIMIT = 56 * 1024 * 1024

M_E1, M_E2, M_R1, M_R2, M_G1, M_G2 = 0, 1, 2, 3, 4, 5
R_GRP0 = N_EXPERTS


def _cparams(sem):
    return pltpu.CompilerParams(dimension_semantics=sem, vmem_limit_bytes=VMEM_LIMIT)


def _mm(a, b):
    return jnp.dot(a.astype(BF16), b.astype(BF16), preferred_element_type=F32)


def _mm_nt(a, b):
    return lax.dot_general(a.astype(BF16), b.astype(BF16), (((1,), (1,)), ((), ())),
                           preferred_element_type=F32)


def _mm_x3(a, b):
    ah = a.astype(BF16)
    al = (a - ah.astype(F32)).astype(BF16)
    bh = b.astype(BF16)
    bl = (b - bh.astype(F32)).astype(BF16)
    return jnp.dot(jnp.concatenate([ah, al, ah], axis=1), jnp.concatenate([bh, bh, bl], axis=0),
                   preferred_element_type=F32)


def _split3(x):
    hi = x.astype(BF16)
    r = x - hi.astype(F32)
    mid = r.astype(BF16)
    lo = (r - mid.astype(F32)).astype(BF16)
    return hi, mid, lo


def _norm_mod(x, nw, sc, sh):
    ms = jnp.mean(x * x, axis=-1, keepdims=True)
    return (x * lax.rsqrt(ms + EPS) * nw) * (1.0 + sc) + sh


def _silu(x):
    return x * jax.nn.sigmoid(x)


def _ada_kernel(ct_ref, w_ref, b_ref, o_ref):
    nb = ct_ref.shape[1]
    d = ct_ref.shape[0]
    rows = []
    for b in range(nb):
        acc = None
        for k0 in range(0, d, LANES):
            ct = ct_ref[k0:k0 + LANES, b:b + 1]
            part = jnp.sum(_silu(ct) * w_ref[0, k0:k0 + LANES, :], axis=0, keepdims=True)
            acc = part if acc is None else acc + part
        rows.append(acc)
    o_ref[0] = jnp.concatenate(rows, axis=0) + b_ref[0]


def _ada(c, ada_w, ada_b):
    nb, d = c.shape
    depth, _, n6 = ada_w.shape
    cols = 1536
    return pl.pallas_call(
        _ada_kernel,
        grid=(depth, n6 // cols),
        in_specs=[
            pl.BlockSpec((d, nb), lambda l, j: (0, 0)),
            pl.BlockSpec((1, d, cols), lambda l, j: (l, 0, j)),
            pl.BlockSpec((1, 1, cols), lambda l, j: (l, 0, j)),
        ],
        out_specs=pl.BlockSpec((1, nb, cols), lambda l, j: (l, 0, j)),
        out_shape=jax.ShapeDtypeStruct((depth, nb, n6), F32),
        compiler_params=_cparams(("arbitrary", "arbitrary")),
        name="ada_mod",
    )(c.T, ada_w, ada_b.reshape(depth, 1, n6))


def _inproj_kernel(x_ref, mod_ref, nw_ref, w_ref, a_ref, b_ref, g_ref):
    h = _norm_mod(x_ref[...], nw_ref[...], mod_ref[0, 1:2, :], mod_ref[0, 0:1, :])
    hb = h.astype(BF16)
    na = a_ref.shape[1]
    nbb = b_ref.shape[1]
    a_ref[...] = jnp.dot(hb, w_ref[:, 0:na], preferred_element_type=F32)
    b_ref[...] = jnp.dot(hb, w_ref[:, na:na + nbb], preferred_element_type=F32).astype(BF16)
    g_ref[...] = jnp.dot(hb, w_ref[:, na + nbb:], preferred_element_type=F32)


def _inproj(x, mod, nw, w_all, seq, tm):
    t, d = x.shape
    tps = seq // tm
    na, nbb, ng = 4 * WA, WB + 2 * WKV, LANES
    return pl.pallas_call(
        _inproj_kernel,
        grid=(t // tm,),
        in_specs=[
            pl.BlockSpec((tm, d), lambda i: (i, 0)),
            pl.BlockSpec((1, 6, d), lambda i: (i // tps, 0, 0)),
            pl.BlockSpec((1, d), lambda i: (0, 0)),
            pl.BlockSpec((d, na + nbb + ng), lambda i: (0, 0)),
        ],
        out_specs=[
            pl.BlockSpec((tm, na), lambda i: (i, 0)),
            pl.BlockSpec((tm, nbb), lambda i: (i, 0)),
            pl.BlockSpec((tm, ng), lambda i: (i, 0)),
        ],
        out_shape=[
            jax.ShapeDtypeStruct((t, na), F32),
            jax.ShapeDtypeStruct((t, nbb), BF16),
            jax.ShapeDtypeStruct((t, ng), F32),
        ],
        compiler_params=_cparams(("arbitrary",)),
        name="inproj",
    )(x, mod, nw, w_all)


def _lane_half(shape):
    return lax.broadcasted_iota(I32, shape, 1) < (LANES // 2)


def _half_sums(x2, lo):
    se = jnp.sum(jnp.where(lo, x2, 0.0), axis=-1, keepdims=True)
    so = jnp.sum(jnp.where(lo, 0.0, x2), axis=-1, keepdims=True)
    return jnp.where(lo, se, so)


def _deltanet_kernel(a_ref, gp_ref, cw_ref, alog_ref, dt_ref, onw_ref, o_ref,
                     ext_ref, s_ref, q_ref, k_ref, v_ref, gf_ref, bf_ref, *, blocks_per_seq):
    lb = a_ref.shape[0]
    i = pl.program_id(0)

    @pl.when(i % blocks_per_seq == 0)
    def _():
        ext_ref[0:8, :] = jnp.zeros((8, ext_ref.shape[1]), F32)
        s_ref[...] = jnp.zeros(s_ref.shape, F32)

    lo_b = _lane_half((lb, LANES))
    for part, dst in ((0, q_ref), (1, k_ref), (2, v_ref)):
        for p in range(N_PAIR):
            c0 = part * WA + p * LANES
            xin = a_ref[:, c0:c0 + LANES]
            ext_ref[8:8 + lb, c0:c0 + LANES] = xin
            acc = cw_ref[CONV_K - 1:CONV_K, c0:c0 + LANES] * xin
            for j in range(CONV_K - 1):
                acc = acc + cw_ref[j:j + 1, c0:c0 + LANES] * ext_ref[pl.ds(8 - (CONV_K - 1) + j, lb), c0:c0 + LANES]
            ext_ref[0:8, c0:c0 + LANES] = xin[lb - 8:lb, :]
            y = _silu(acc)
            if part < 2:
                y = y * lax.rsqrt(_half_sums(y * y, lo_b) + EPS)
                if part == 0:
                    y = y * (DK_A ** -0.5)
            dst[:, p * LANES:(p + 1) * LANES] = y

    gp = gp_ref[...]
    bf_ref[...] = jax.nn.sigmoid(gp)
    xg = gp + dt_ref[...]
    softplus = jnp.maximum(xg, 0.0) + jnp.log(1.0 + jnp.exp(-jnp.abs(xg)))
    gf_ref[...] = -jnp.exp(alog_ref[...]) * softplus

    c = CHUNK
    c2 = 2 * c
    row = lax.broadcasted_iota(I32, (c2, c2), 0)
    col = lax.broadcasted_iota(I32, (c2, c2), 1)
    same_blk = (row < c) == (col < c)
    incl = same_blk & (row >= col)
    strict = same_blk & (row > col)
    eye = row == col
    eye_f = jnp.where(eye, 1.0, 0.0)
    lo = _lane_half((c, LANES))

    def stack(x):
        return jnp.where(same_blk, jnp.concatenate([x, x], axis=0), 0.0)

    def unstack(x):
        return jnp.where(lo, x[0:c], x[c:c2])

    rb = lax.broadcasted_iota(I32, (lb, lb), 0)
    cb = lax.broadcasted_iota(I32, (lb, lb), 1)
    shift = CHUNK.bit_length() - 1
    tril_chunks = jnp.where((jnp.right_shift(rb, shift) == jnp.right_shift(cb, shift)) & (rb >= cb),
                            1.0, 0.0).astype(BF16)
    g_hi, g_mid, g_lo = _split3(gf_ref[...])
    gf_ref[...] = (jnp.dot(tril_chunks, g_hi, preferred_element_type=F32)
                   + jnp.dot(tril_chunks, g_mid, preferred_element_type=F32)
                   + jnp.dot(tril_chunks, g_lo, preferred_element_type=F32))

    pairs = range(N_PAIR)

    def chunk_body(ci, carry):
        r0 = pl.multiple_of(ci * c, c)
        gcc = gf_ref[pl.ds(r0, c), :]
        bfc = bf_ref[pl.ds(r0, c), :]
        q = [q_ref[pl.ds(r0, c), p * LANES:(p + 1) * LANES] for p in pairs]
        k = [k_ref[pl.ds(r0, c), p * LANES:(p + 1) * LANES] for p in pairs]
        v = [v_ref[pl.ds(r0, c), p * LANES:(p + 1) * LANES] for p in pairs]

        def lane_bcast(x, j):
            return jnp.broadcast_to(x[:, j:j + 1], (c, LANES))

        beta = [jnp.where(lo, lane_bcast(bfc, 2 * p), lane_bcast(bfc, 2 * p + 1)) for p in pairs]
        gce = [lane_bcast(gcc, H_A + 2 * p) for p in pairs]
        gco = [lane_bcast(gcc, H_A + 2 * p + 1) for p in pairs]
        gcol = [jnp.concatenate([gce[p], gco[p]], axis=0) for p in pairs]
        gc = [jnp.where(lo, gce[p], gco[p]) for p in pairs]
        grow = [jnp.sum(jnp.where(eye, gcol[p], 0.0), axis=0, keepdims=True) for p in pairs]
        decay = [jnp.where(incl, jnp.exp(jnp.where(incl, gcol[p] - grow[p], 0.0)), 0.0) for p in pairs]
        egc = [jnp.exp(gc[p]) for p in pairs]
        kb = [k[p] * beta[p] for p in pairs]
        k_st = [stack(k[p]) for p in pairs]
        n_mat = [jnp.where(strict, _mm_nt(stack(kb[p]), k_st[p]) * decay[p], 0.0) for p in pairs]
        a_qk = [_mm_nt(stack(q[p]), k_st[p]) * decay[p] for p in pairs]
        x_inv = [eye_f - n_mat[p] for p in pairs]
        y_pow = [_mm_x3(n_mat[p], n_mat[p]) for p in pairs]
        for it in range(5):
            x_inv = [x_inv[p] + _mm_x3(x_inv[p], y_pow[p]) for p in pairs]
            if it < 4:
                y_pow = [_mm_x3(y_pow[p], y_pow[p]) for p in pairs]
        sol = []
        for p in pairs:
            vb = v[p] * beta[p]
            kbg = kb[p] * egc[p]
            rhs = jnp.concatenate([jnp.concatenate([vb, vb], axis=0),
                                   jnp.concatenate([kbg, kbg], axis=0)], axis=1)
            sol.append(_mm(x_inv[p], rhs))
        u = [unstack(sol[p][:, 0:LANES]) for p in pairs]
        w = [unstack(sol[p][:, LANES:2 * LANES]) for p in pairs]
        state = [s_ref[p] for p in pairs]
        wq = [_mm(jnp.concatenate([w[p], q[p] * egc[p]], axis=0), state[p]) for p in pairs]
        v_new = [u[p] - wq[p][0:c] for p in pairs]
        av = [_mm(a_qk[p], jnp.concatenate([v_new[p], v_new[p]], axis=0)) for p in pairs]
        for p in pairs:
            k_dec = k[p] * jnp.exp(gc[p][c - 1:c, :] - gc[p])
            g_tot = jnp.where(row < c, jnp.exp(gcol[p][c - 1:c, :]), jnp.exp(gcol[p][c2 - 1:c2, :]))
            s_ref[p] = state[p] * g_tot + jnp.where(same_blk, _mm(k_dec.T, v_new[p]), 0.0)
        for p in pairs:
            out = wq[p][c:c2] + unstack(av[p])
            ms = _half_sums(out * out, lo) * (1.0 / DK_A)
            z = a_ref[pl.ds(r0, c), 3 * WA + p * LANES:3 * WA + (p + 1) * LANES]
            res = out * lax.rsqrt(ms + EPS) * onw_ref[...] * _silu(z)
            o_ref[pl.ds(r0, c), p * LANES:(p + 1) * LANES] = res.astype(BF16)
        return carry

    lax.fori_loop(0, lb // c, chunk_body, 0, unroll=True)


def _deltanet(a1, g1, conv_w, alog_row, dt_row, onw_row, seq, lb):
    t = a1.shape[0]
    kern = functools.partial(_deltanet_kernel, blocks_per_seq=seq // lb)
    return pl.pallas_call(
        kern,
        grid=(t // lb,),
        in_specs=[
            pl.BlockSpec((lb, 4 * WA), lambda i: (i, 0)),
            pl.BlockSpec((lb, LANES), lambda i: (i, 0)),
            pl.BlockSpec((CONV_K, 3 * WA), lambda i: (0, 0)),
            pl.BlockSpec((1, LANES), lambda i: (0, 0)),
            pl.BlockSpec((1, LANES), lambda i: (0, 0)),
            pl.BlockSpec((1, LANES), lambda i: (0, 0)),
        ],
        out_specs=pl.BlockSpec((lb, WA), lambda i: (i, 0)),
        out_shape=jax.ShapeDtypeStruct((t, WA), BF16),
        scratch_shapes=[
            pltpu.VMEM((lb + 8, 3 * WA), F32),
            pltpu.VMEM((N_PAIR, LANES, LANES), F32),
            pltpu.VMEM((lb, WA), F32),
            pltpu.VMEM((lb, WA), F32),
            pltpu.VMEM((lb, WA), F32),
            pltpu.VMEM((lb, LANES), F32),
            pltpu.VMEM((lb, LANES), F32),
        ],
        compiler_params=_cparams(("arbitrary",)),
        name="deltanet",
    )(a1, g1, conv_w, alog_row, dt_row, onw_row)


def _swa_kernel(sink_ref, cur_ref, prev_ref, bias_ref, o_ref, *, blocks_per_seq):
    blk = ATT_BLK
    i = pl.program_id(0)
    first = (i % blocks_per_seq) == 0
    row = lax.broadcasted_iota(I32, (2 * blk, 2 * blk), 0)
    col = lax.broadcasted_iota(I32, (2 * blk, 2 * blk), 1)
    qi = jnp.where(row < blk, row, row - blk)
    dist = qi + blk - col
    first_key = jnp.where(first, blk, 0)
    valid = (dist >= 0) & (dist < WINDOW) & (col >= first_key)
    lo_kv = _lane_half((2 * blk, LANES))
    lo_q = _lane_half((blk, LANES))
    rowq = lax.broadcasted_iota(I32, (2 * blk, LANES), 0)
    q_keep = (rowq < blk) == lo_kv
    top_rows = lax.broadcasted_iota(I32, (2 * blk, 1), 0) < blk

    k0 = WB
    v0 = WB + WKV
    kk = jnp.concatenate([prev_ref[:, k0:k0 + WKV], cur_ref[:, k0:k0 + WKV]], axis=0).astype(F32)
    vv = jnp.concatenate([prev_ref[:, v0:v0 + WKV], cur_ref[:, v0:v0 + WKV]], axis=0).astype(F32)
    kk_sw = pltpu.roll(kk, LANES // 2, axis=1)
    vv_sw = pltpu.roll(vv, LANES // 2, axis=1)
    k_dup = (jnp.where(lo_kv, kk, kk_sw), jnp.where(lo_kv, kk_sw, kk))
    v_dup = (jnp.where(lo_kv, vv, vv_sw), jnp.where(lo_kv, vv_sw, vv))

    for p in range(H_B // 2):
        j = (2 * p) // (H_B // H_B_KV)
        qp = cur_ref[:, p * LANES:(p + 1) * LANES].astype(F32) * (DH_B ** -0.5)
        q_st = jnp.where(q_keep, jnp.concatenate([qp, qp], axis=0), 0.0)
        s = _mm_nt(q_st, k_dup[j])
        s = jnp.where(valid, s + bias_ref[p], -1e30)
        sink = jnp.where(top_rows, sink_ref[2 * p], sink_ref[2 * p + 1])
        m = jnp.maximum(jnp.max(s, axis=-1, keepdims=True), sink)
        pe = jnp.exp(s - m)
        den = jnp.sum(pe, axis=-1, keepdims=True) + jnp.exp(sink - m)
        o2 = _mm(pe, v_dup[j]) / den
        o_ref[:, p * LANES:(p + 1) * LANES] = jnp.where(lo_q, o2[0:blk], o2[blk:2 * blk]).astype(BF16)


def _swa(b1, sinks, bias_tab, seq):
    t, nc = b1.shape
    nbs = seq // ATT_BLK
    kern = functools.partial(_swa_kernel, blocks_per_seq=nbs)
    return pl.pallas_call(
        kern,
        grid=(t // ATT_BLK,),
        in_specs=[
            pl.BlockSpec(memory_space=pltpu.SMEM),
            pl.BlockSpec((ATT_BLK, nc), lambda i: (i, 0)),
            pl.BlockSpec((ATT_BLK, nc), lambda i: (jnp.maximum(i - 1, 0), 0)),
            pl.BlockSpec((H_B // 2, 2 * ATT_BLK, 2 * ATT_BLK), lambda i: (0, 0, 0)),
        ],
        out_specs=pl.BlockSpec((ATT_BLK, WB), lambda i: (i, 0)),
        out_shape=jax.ShapeDtypeStruct((t, WB), BF16),
        compiler_params=_cparams(("arbitrary",)),
        name="swa",
    )(sinks, b1, b1, bias_tab)


def _t5_bias_table(rel_bias):
    qi = jnp.arange(ATT_BLK)[:, None]
    ki = jnp.arange(2 * ATT_BLK)[None, :]
    n = jnp.maximum(qi + ATT_BLK - ki, 0)
    max_exact = NUM_BUCKETS // 2
    nf = jnp.maximum(n, 1).astype(F32)
    large = max_exact + (jnp.log(nf / max_exact) / math.log(MAX_DISTANCE / max_exact)
                         * (NUM_BUCKETS - max_exact)).astype(I32)
    bucket = jnp.where(n < max_exact, n, jnp.minimum(large, NUM_BUCKETS - 1))
    onehot = (bucket[..., None] == jnp.arange(NUM_BUCKETS)).astype(F32)
    bias = jnp.einsum('qkb,bh->hqk', onehot, rel_bias.astype(F32), precision=lax.Precision.HIGHEST)
    return bias.reshape(H_B // 2, 2 * ATT_BLK, 2 * ATT_BLK)


def _route_tail(x1, mod_ref, nw2_ref, rwhl_ref, rwhi_ref, rb_ref, tril_ref, carry_ref,
                h2_ref, meta_ref, metat_ref, cnt_ref):
    tm = x1.shape[0]
    h2 = _norm_mod(x1, nw2_ref[...], mod_ref[0, 4:5, :], mod_ref[0, 3:4, :])
    _store_row_tiles(h2_ref, h2)
    h_hi = h2.astype(BF16)
    h_lo = (h2 - h_hi.astype(F32)).astype(BF16)
    r = jnp.dot(h_hi, rwhl_ref[...], preferred_element_type=F32)
    logits = (r[:, 0:LANES] + r[:, LANES:2 * LANES]
              + jnp.dot(h_lo, rwhi_ref[...], preferred_element_type=F32) + rb_ref[...])
    lane = lax.broadcasted_iota(I32, (tm, LANES), 1)
    lane_f = lane.astype(F32)
    big = 1e9
    neg = -jnp.inf
    is_grp = (lane >= R_GRP0) & (lane < R_GRP0 + N_GROUPS)
    lg1 = jnp.where(is_grp, logits, neg)
    m1 = jnp.max(lg1, axis=-1, keepdims=True)
    grp = jnp.min(jnp.where(lg1 == m1, lane_f, big), axis=-1, keepdims=True) - R_GRP0
    p_top = 1.0 / jnp.sum(jnp.where(is_grp, jnp.exp(lg1 - m1), 0.0), axis=-1, keepdims=True)
    lane_grp = jnp.right_shift(lane, EXPERTS_PER_GROUP.bit_length() - 1).astype(F32)
    in_grp = (lane < N_EXPERTS) & (lane_grp == grp)
    l2 = jnp.where(in_grp, logits, neg)
    t1 = jnp.max(l2, axis=-1, keepdims=True)
    e1 = jnp.min(jnp.where(l2 == t1, lane_f, big), axis=-1, keepdims=True)
    l2b = jnp.where(lane_f == e1, neg, l2)
    t2 = jnp.max(l2b, axis=-1, keepdims=True)
    e2 = jnp.min(jnp.where(l2b == t2, lane_f, big), axis=-1, keepdims=True)
    ex = jnp.exp(t2 - t1)
    gate1 = p_top / (1.0 + ex)
    gate2 = p_top * ex / (1.0 + ex)
    hit1 = lane_f == e1
    hit2 = lane_f == e2
    onehot = jnp.where(hit1 | hit2, 1.0, 0.0)
    incl = jnp.dot(tril_ref[...], onehot.astype(BF16), preferred_element_type=F32)
    excl = incl - onehot + carry_ref[...]
    rank1 = jnp.sum(jnp.where(hit1, excl, 0.0), axis=-1, keepdims=True)
    rank2 = jnp.sum(jnp.where(hit2, excl, 0.0), axis=-1, keepdims=True)
    carry_ref[...] = carry_ref[...] + incl[tm - 1:tm, :]
    cnt_ref[...] = carry_ref[...]
    meta = jnp.zeros((tm, LANES), F32)
    for idx, val in ((M_E1, e1), (M_E2, e2), (M_R1, rank1), (M_R2, rank2), (M_G1, gate1), (M_G2, gate2)):
        meta = jnp.where(lane == idx, val, meta)
    meta_ref[...] = meta
    metat_ref[0] = meta.T[0:8, :]


def _chunk_rows(cidx, n):
    return pl.ds(cidx, n, stride=ROW_TILE)


def _store_row_tiles(ref, x):
    for cidx in range(ROW_TILE):
        ref[_chunk_rows(cidx, x.shape[0]), :] = x[:, cidx * LANES:(cidx + 1) * LANES]


def _load_row_tiles(ref):
    n = ref.shape[0] // ROW_TILE
    return jnp.concatenate([ref[_chunk_rows(cidx, n), :] for cidx in range(ROW_TILE)], axis=1)


def _tail_in_specs(d, tm, tps):
    return [
        pl.BlockSpec((1, 6, d), lambda i: (i // tps, 0, 0)),
        pl.BlockSpec((1, d), lambda i: (0, 0)),
        pl.BlockSpec((d, 2 * LANES), lambda i: (0, 0)),
        pl.BlockSpec((d, LANES), lambda i: (0, 0)),
        pl.BlockSpec((1, LANES), lambda i: (0, 0)),
        pl.BlockSpec((tm, tm), lambda i: (0, 0)),
    ]


def _tail_out(t, d, tm):
    specs = [
        pl.BlockSpec((tm, d), lambda i: (i, 0)),
        pl.BlockSpec((tm * ROW_TILE, LANES), lambda i: (i, 0)),
        pl.BlockSpec((tm, LANES), lambda i: (i, 0)),
        pl.BlockSpec((1, 8, tm), lambda i: (i, 0, 0)),
        pl.BlockSpec((1, LANES), lambda i: (0, 0)),
    ]
    shapes = [
        jax.ShapeDtypeStruct((t, d), F32),
        jax.ShapeDtypeStruct((t * ROW_TILE, LANES), F32),
        jax.ShapeDtypeStruct((t, LANES), F32),
        jax.ShapeDtypeStruct((t // tm, 8, tm), F32),
        jax.ShapeDtypeStruct((1, LANES), F32),
    ]
    return specs, shapes


def _outproj_kernel(oa_ref, ob_ref, x_ref, wo_ref, mod_ref, nw2_ref, rwhl_ref, rwhi_ref, rb_ref,
                    tril_ref, x1_ref, h2_ref, meta_ref, metat_ref, cnt_ref, carry_ref):
    @pl.when(pl.program_id(0) == 0)
    def _():
        carry_ref[...] = jnp.zeros(carry_ref.shape, F32)

    y = (jnp.dot(oa_ref[...], wo_ref[0:WA, :], preferred_element_type=F32)
         + jnp.dot(ob_ref[...], wo_ref[WA:WA + WB, :], preferred_element_type=F32))
    x1 = x_ref[...] + mod_ref[0, 2:3, :] * y
    x1_ref[...] = x1
    _route_tail(x1, mod_ref, nw2_ref, rwhl_ref, rwhi_ref, rb_ref, tril_ref, carry_ref,
                h2_ref, meta_ref, metat_ref, cnt_ref)


def _outproj_route(oa, ob, x, w_out_b, mod, nw2, rwhl, rwhi, rb, tril, seq, tm):
    t, d = x.shape
    tps = seq // tm
    out_specs, out_shapes = _tail_out(t, d, tm)
    return pl.pallas_call(
        _outproj_kernel,
        grid=(t // tm,),
        in_specs=[
            pl.BlockSpec((tm, WA), lambda i: (i, 0)),
            pl.BlockSpec((tm, WB), lambda i: (i, 0)),
            pl.BlockSpec((tm, d), lambda i: (i, 0)),
            pl.BlockSpec((WA + WB, d), lambda i: (0, 0)),
        ] + _tail_in_specs(d, tm, tps),
        out_specs=out_specs,
        out_shape=out_shapes,
        scratch_shapes=[pltpu.VMEM((1, LANES), F32)],
        compiler_params=_cparams(("arbitrary",)),
        name="outproj_route",
    )(oa, ob, x, w_out_b, mod, nw2, rwhl, rwhi, rb, tril)


def _pool_kernel(x_ref, nw1_ref, pw_ref, ps_ref, mod_ref, nw2_ref, rwhl_ref, rwhi_ref, rb_ref,
                 tril_ref, x1_ref, h2_ref, meta_ref, metat_ref, cnt_ref, carry_ref, e1_ref, e2_ref, e3_ref,
                 *, blocks_per_seq):
    tp, d = x_ref.shape
    hist = POOL_HIST
    i = pl.program_id(0)

    @pl.when(i == 0)
    def _():
        carry_ref[...] = jnp.zeros(carry_ref.shape, F32)

    @pl.when(i % blocks_per_seq == 0)
    def _():
        e1_ref[0:hist, :] = jnp.zeros((hist, d), F32)

    x = x_ref[...]
    h = _norm_mod(x, nw1_ref[...], mod_ref[0, 1:2, :], mod_ref[0, 0:1, :])
    e1_ref[hist:hist + tp, :] = h
    pos = (i % blocks_per_seq) * tp + lax.broadcasted_iota(I32, (tp, 1), 0)
    gdim = POOL_GROUP_DIM
    ys = []
    for gi, win in enumerate(POOL_WINDOWS):
        c0 = gi * gdim
        levels = win.bit_length() - 1
        src, spare = e1_ref, (e2_ref, e3_ref)
        for m in range(1, levels + 1):
            shift = 1 << (m - 1)
            start = hist - 8 * (levels - m)
            n = hist + tp - start
            cur = src[start:start + n, c0:c0 + gdim] + src[start - shift:start - shift + n, c0:c0 + gdim]
            if m < levels:
                dst = spare[m % 2]
                dst[start:start + n, c0:c0 + gdim] = cur
                src = dst
        cnt = jnp.minimum(pos + 1, win).astype(F32)
        pooled = cur / cnt - h[:, c0:c0 + gdim]
        ys.append(_mm(pooled, pw_ref[gi]))
    e1_ref[0:hist, :] = h[tp - hist:tp, :]
    y = jnp.concatenate(ys, axis=1) * ps_ref[...]
    x1 = x + mod_ref[0, 2:3, :] * y
    x1_ref[...] = x1
    _route_tail(x1, mod_ref, nw2_ref, rwhl_ref, rwhi_ref, rb_ref, tril_ref, carry_ref,
                h2_ref, meta_ref, metat_ref, cnt_ref)


def _pool_route(x, nw1, pool_w_b, pool_scale, mod, nw2, rwhl, rwhi, rb, tril, seq, tp):
    t, d = x.shape
    tps = seq // tp
    out_specs, out_shapes = _tail_out(t, d, tp)
    kern = functools.partial(_pool_kernel, blocks_per_seq=tps)
    return pl.pallas_call(
        kern,
        grid=(t // tp,),
        in_specs=[
            pl.BlockSpec((tp, d), lambda i: (i, 0)),
            pl.BlockSpec((1, d), lambda i: (0, 0)),
            pl.BlockSpec((4, POOL_GROUP_DIM, POOL_GROUP_DIM), lambda i: (0, 0, 0)),
            pl.BlockSpec((1, d), lambda i: (0, 0)),
        ] + _tail_in_specs(d, tp, tps),
        out_specs=out_specs,
        out_shape=out_shapes,
        scratch_shapes=[
            pltpu.VMEM((1, LANES), F32),
            pltpu.VMEM((tp + POOL_HIST, d), F32),
            pltpu.VMEM((tp + POOL_HIST, d), F32),
            pltpu.VMEM((tp + POOL_HIST, d), F32),
        ],
        compiler_params=_cparams(("arbitrary",)),
        name="pool_route",
    )(x, nw1, pool_w_b, pool_scale, mod, nw2, rwhl, rwhi, rb, tril)


def _dispatch_kernel(dest_ref, fill_ref, h_ref, xs_ref, zero_ref, sem):
    td = h_ref.shape[0] // ROW_TILE

    def row_tile(r, n=1):
        return pl.ds(pl.multiple_of(r * ROW_TILE, ROW_TILE), n * ROW_TILE)

    @pl.when(pl.program_id(0) == 0)
    def _():
        zero_ref[...] = jnp.zeros(zero_ref.shape, zero_ref.dtype)

        def fill(e, carry):
            pos = fill_ref[0, e]
            n = fill_ref[1, e]
            sizes = [1 << bit for bit in reversed(range(MOE_ROWS.bit_length() - 1))]
            for size in sizes:
                take = (n & size) != 0

                @pl.when(take)
                def _():
                    pltpu.make_async_copy(zero_ref.at[row_tile(0, size)], xs_ref.at[row_tile(pos, size)], sem).start()

                pos = pos + jnp.where(take, size, 0)
            for size in sizes:
                @pl.when((n & size) != 0)
                def _():
                    pltpu.make_async_copy(zero_ref.at[row_tile(0, size)], xs_ref.at[row_tile(0, size)], sem).wait()

            return carry

        lax.fori_loop(0, N_EXPERTS, fill, 0)

        half = MOE_ROWS // 2

        def fill_tail(b, carry):
            for part in range(2):
                pltpu.make_async_copy(zero_ref, xs_ref.at[row_tile(b * MOE_ROWS + part * half, half)], sem).start()
            return carry

        def wait_tail(b, carry):
            for part in range(2):
                pltpu.make_async_copy(zero_ref, xs_ref.at[row_tile(0, half)], sem).wait()
            return carry

        n_blk = xs_ref.shape[0] // (ROW_TILE * MOE_ROWS)
        lax.fori_loop(fill_ref[0, N_EXPERTS], n_blk, fill_tail, 0)
        lax.fori_loop(fill_ref[0, N_EXPERTS], n_blk, wait_tail, 0)

    def issue(tk, carry):
        for s in range(2):
            d = dest_ref[0, 0, s * td + tk]
            pltpu.make_async_copy(h_ref.at[row_tile(tk)], xs_ref.at[row_tile(d)], sem).start(priority=s)
        return carry

    lax.fori_loop(0, td, issue, 0)

    for s in range(2):
        pltpu.make_async_copy(h_ref, xs_ref.at[row_tile(0, td)], sem).wait()


def _dispatch(dest3, fill, h2, n_rows, td):
    t = h2.shape[0] // ROW_TILE
    return pl.pallas_call(
        _dispatch_kernel,
        grid=(t // td,),
        in_specs=[
            pl.BlockSpec((1, 1, 2 * td), lambda i: (i, 0, 0), memory_space=pltpu.SMEM),
            pl.BlockSpec(memory_space=pltpu.SMEM),
            pl.BlockSpec((td * ROW_TILE, LANES), lambda i: (i, 0)),
        ],
        out_specs=pl.BlockSpec(memory_space=pl.ANY),
        out_shape=jax.ShapeDtypeStruct((n_rows * ROW_TILE, LANES), h2.dtype),
        scratch_shapes=[pltpu.VMEM((MOE_ROWS // 2 * ROW_TILE, LANES), h2.dtype), pltpu.SemaphoreType.DMA(())],
        compiler_params=_cparams(("arbitrary",)),
        name="moe_dispatch",
    )(dest3, fill, h2)


def _expert_kernel(be_ref, nu_ref, xs_hbm, wg_ref, wu_ref, wd_ref, y_ref, wgb_ref, wub_ref, wdb_ref,
                   ring_ref, sems):
    i = pl.program_id(0)
    n_blk = pl.num_programs(0)
    blk_rows = MOE_ROWS * ROW_TILE

    def fetch(step):
        slot = step % XS_RING
        start = step * blk_rows
        if not isinstance(step, int):
            start = pl.multiple_of(start, blk_rows)
        src = xs_hbm.at[pl.ds(start, blk_rows)]
        return pltpu.make_async_copy(src, ring_ref.at[slot], sems.at[slot])

    @pl.when(i == 0)
    def _():
        for step in range(XS_RING - 1):
            fetch(step).start()

    @pl.when(i + XS_RING - 1 < n_blk)
    def _():
        fetch(i + XS_RING - 1).start()

    fetch(i).wait()
    xs_ref = ring_ref.at[i % XS_RING]
    prev = be_ref[jnp.maximum(i - 1, 0)]
    changed = (i == 0) | (be_ref[i] != prev)

    @pl.when(changed)
    def _():
        wgb_ref[...] = wg_ref[0, 0].astype(BF16)
        wub_ref[...] = wu_ref[0, 0].astype(BF16)
        wdb_ref[...] = wd_ref[0, 0].astype(BF16)

    @pl.when(i < nu_ref[0])
    def _():
        xb = _load_row_tiles(xs_ref).astype(BF16)
        g = jnp.dot(xb, wgb_ref[...], preferred_element_type=F32)
        u = jnp.dot(xb, wub_ref[...], preferred_element_type=F32)
        a = (_silu(g) * u).astype(BF16)
        _store_row_tiles(y_ref, jnp.dot(a, wdb_ref[...], preferred_element_type=F32))

    @pl.when(i >= nu_ref[0])
    def _():
        y_ref[...] = jnp.zeros(y_ref.shape, F32)


def _experts(blk_expert, n_used, xs, w_gate, w_up, w_down, layer):
    n_rows = xs.shape[0] // ROW_TILE
    d = ROW_TILE * LANES
    blk = (MOE_ROWS * ROW_TILE, LANES)
    n_blk = n_rows // MOE_ROWS
    return pl.pallas_call(
        _expert_kernel,
        grid_spec=pltpu.PrefetchScalarGridSpec(
            num_scalar_prefetch=2,
            grid=(n_blk,),
            in_specs=[
                pl.BlockSpec(memory_space=pl.ANY),
                pl.BlockSpec((1, 1, d, D_FF), lambda i, be, nu: (layer, be[i], 0, 0)),
                pl.BlockSpec((1, 1, d, D_FF), lambda i, be, nu: (layer, be[i], 0, 0)),
                pl.BlockSpec((1, 1, D_FF, d), lambda i, be, nu: (layer, be[i], 0, 0)),
            ],
            out_specs=pl.BlockSpec(blk, lambda i, be, nu: (i, 0)),
            scratch_shapes=[
                pltpu.VMEM((d, D_FF), BF16),
                pltpu.VMEM((d, D_FF), BF16),
                pltpu.VMEM((D_FF, d), BF16),
                pltpu.VMEM((XS_RING,) + blk, F32),
                pltpu.SemaphoreType.DMA((XS_RING,)),
            ],
        ),
        out_shape=jax.ShapeDtypeStruct(xs.shape, F32),
        compiler_params=_cparams(("arbitrary",)),
        name="moe_experts",
    )(blk_expert, n_used, xs, w_gate, w_up, w_down)


def _combine_kernel(dest_ref, ys_ref, x1_ref, meta_ref, mod_ref, fw_ref, o_ref, buf_ref, sem, *, final_norm):
    tc = x1_ref.shape[0]

    def row_tile(r):
        return pl.ds(pl.multiple_of(r * ROW_TILE, ROW_TILE), ROW_TILE)

    def issue(tk, carry):
        for s in range(2):
            d = dest_ref[0, 0, s * tc + tk]
            pltpu.make_async_copy(ys_ref.at[row_tile(d)], buf_ref.at[s, row_tile(tk)], sem).start(priority=s)
        return carry

    lax.fori_loop(0, tc, issue, 0)

    for s in range(2):
        pltpu.make_async_copy(ys_ref.at[pl.ds(0, tc * ROW_TILE)], buf_ref.at[s], sem).wait()

    meta = meta_ref[...]
    g1 = meta[:, M_G1:M_G1 + 1]
    g2 = meta[:, M_G2:M_G2 + 1]
    n_chunks = ROW_TILE
    ssq = jnp.zeros((tc, 1), F32)
    for cidx in range(n_chunks):
        sl = slice(cidx * LANES, (cidx + 1) * LANES)
        rows = _chunk_rows(cidx, tc)
        moe = buf_ref[0, rows, :] * g1 + buf_ref[1, rows, :] * g2
        x2 = x1_ref[:, sl] + mod_ref[0, 5:6, sl] * moe
        o_ref[:, sl] = x2
        if final_norm:
            ssq = ssq + jnp.sum(x2 * x2, axis=-1, keepdims=True)
    if final_norm:
        inv = lax.rsqrt(ssq * (1.0 / (n_chunks * LANES)) + EPS)
        for cidx in range(n_chunks):
            sl = slice(cidx * LANES, (cidx + 1) * LANES)
            o_ref[:, sl] = o_ref[:, sl] * inv * fw_ref[:, sl]


def _combine(dest3, ys, x1, meta, mod, fw, seq, tc, final_norm):
    t, d = x1.shape
    tps = seq // tc
    kern = functools.partial(_combine_kernel, final_norm=final_norm)
    return pl.pallas_call(
        kern,
        grid=(t // tc,),
        in_specs=[
            pl.BlockSpec((1, 1, 2 * tc), lambda i: (i, 0, 0), memory_space=pltpu.SMEM),
            pl.BlockSpec(memory_space=pl.ANY),
            pl.BlockSpec((tc, d), lambda i: (i, 0)),
            pl.BlockSpec((tc, LANES), lambda i: (i, 0)),
            pl.BlockSpec((1, 6, d), lambda i: (i // tps, 0, 0)),
            pl.BlockSpec((1, d), lambda i: (0, 0)),
        ],
        out_specs=pl.BlockSpec((tc, d), lambda i: (i, 0)),
        out_shape=jax.ShapeDtypeStruct((t, d), F32),
        scratch_shapes=[pltpu.VMEM((2, tc * ROW_TILE, LANES), F32), pltpu.SemaphoreType.DMA(())],
        compiler_params=_cparams(("arbitrary",)),
        name="moe_combine",
    )(dest3, ys, x1, meta, mod, fw)


def _moe(x1, h2, meta, metat, cnt, mod, fw, w_gate, w_up, w_down, layer, seq, tile, final_norm):
    t, d = x1.shape
    n_rows = -(-(2 * t + N_EXPERTS * (MOE_ROWS - 1)) // MOE_ROWS) * MOE_ROWS
    n_blk = n_rows // MOE_ROWS
    counts = cnt[0, :N_EXPERTS].astype(I32)
    padded = (counts + MOE_ROWS - 1) // MOE_ROWS * MOE_ROWS
    pad_end = jnp.cumsum(padded)
    pad_start = pad_end - padded
    expert = metat[:, M_E1:M_E2 + 1, :].astype(I32)
    rank = metat[:, M_R1:M_R2 + 1, :].astype(I32)
    start_of = jnp.sum(jnp.where(expert[..., None] == jnp.arange(N_EXPERTS, dtype=I32), pad_start, 0), axis=-1)
    dest3 = (start_of + rank).reshape(t // tile, 1, 2 * tile)
    blk_row0 = jnp.arange(n_blk, dtype=I32) * MOE_ROWS
    blk_expert = jnp.minimum(jnp.sum((pad_end[None, :] <= blk_row0[:, None]).astype(I32), axis=1),
                             N_EXPERTS - 1)
    n_used = (pad_end[-1:] // MOE_ROWS).astype(I32)
    fill = jnp.stack([jnp.concatenate([pad_start + counts, n_used]),
                      jnp.concatenate([padded - counts, jnp.zeros((1,), I32)])])
    xs = _dispatch(dest3, fill, h2, n_rows, tile)
    ys = _experts(blk_expert, n_used, xs, w_gate, w_up, w_down, layer)
    return _combine(dest3, ys, x1, meta, mod, fw, seq, tile, final_norm)


def _router_params(r1_w, r1_b, r2_w, r2_b):
    d = r1_w.shape[0]
    rw = jnp.zeros((d, LANES), F32).at[:, :N_EXPERTS].set(r2_w).at[:, R_GRP0:R_GRP0 + N_GROUPS].set(r1_w)
    rb = jnp.zeros((1, LANES), F32).at[0, :N_EXPERTS].set(r2_b).at[0, R_GRP0:R_GRP0 + N_GROUPS].set(r1_b)
    hi = rw.astype(BF16)
    lo = (rw - hi.astype(F32)).astype(BF16)
    return jnp.concatenate([hi, lo], axis=1), hi, rb


def kernel(x, c, ada_w, ada_b, norm_mix_w, norm_ffn_w, ab_w_in, ab_conv_w, ab_a_log, ab_dt_bias, ab_onorm_w, ab_sinks, rel_bias, ab_w_out, pool_w, pool_scale, r1_w, r1_b, r2_w, r2_b, moe_w_gate, moe_w_up, moe_w_down, final_norm_w):
    bsz, seq, d = x.shape
    t = bsz * seq
    tile = min(512, seq)
    lb = min(256, seq)
    xf = x.reshape(t, d)

    mod_all = _ada(c, ada_w, ada_b).reshape(DEPTH, bsz, 6, d)
    tril = jnp.tril(jnp.ones((tile, tile), BF16))
    fw = final_norm_w.reshape(1, d)

    w_in = ab_w_in[0]
    cuts = (0, WA, 2 * WA, 3 * WA, 4 * WA, 4 * WA + H_A, 4 * WA + 2 * H_A)
    qkvz = w_in[:, :cuts[4]]
    ba = w_in[:, cuts[4]:cuts[6]]
    qkv_b = w_in[:, cuts[6]:]
    w_all = jnp.concatenate([qkvz, qkv_b, ba, jnp.zeros((d, LANES - 2 * H_A), F32)], axis=1).astype(BF16)
    a1, b1, g1 = _inproj(xf, mod_all[0], norm_mix_w[0:1], w_all, seq, tile)

    pad = jnp.zeros((LANES - 2 * H_A,), F32)
    alog_row = jnp.concatenate([jnp.zeros((H_A,), F32), ab_a_log[0], pad]).reshape(1, LANES)
    dt_row = jnp.concatenate([jnp.zeros((H_A,), F32), ab_dt_bias[0], pad]).reshape(1, LANES)
    onw_row = jnp.concatenate([ab_onorm_w[0], ab_onorm_w[0]]).reshape(1, LANES)
    oa = _deltanet(a1, g1, ab_conv_w[0], alog_row, dt_row, onw_row, seq, lb)
    ob = _swa(b1, ab_sinks[0], _t5_bias_table(rel_bias), seq)

    rwhl, rwhi, rb = _router_params(r1_w[0], r1_b[0], r2_w[0], r2_b[0])
    x1, h2, meta, metat, cnt = _outproj_route(oa, ob, xf, ab_w_out[0].astype(BF16), mod_all[0], norm_ffn_w[0:1],
                                       rwhl, rwhi, rb, tril, seq, tile)
    x2 = _moe(x1, h2, meta, metat, cnt, mod_all[0], fw, moe_w_gate, moe_w_up, moe_w_down, 0,
              seq, tile, False)

    rwhl, rwhi, rb = _router_params(r1_w[1], r1_b[1], r2_w[1], r2_b[1])
    x3, h2, meta, metat, cnt = _pool_route(x2, norm_mix_w[1:2], pool_w[0].astype(BF16), pool_scale[0].reshape(1, d),
                                    mod_all[1], norm_ffn_w[1:2], rwhl, rwhi, rb, tril, seq, tile)
    out = _moe(x3, h2, meta, metat, cnt, mod_all[1], fw, moe_w_gate, moe_w_up, moe_w_down, 1,
               seq, tile, True)
    return out.reshape(bsz, seq, d)
```
